```python
import math
import jax
import jax.numpy as jnp
from jax import lax
import numpy as np

D_MODEL = 1024
BATCH = 8
SEQ = 2048
DEPTH = 1

EPS = 1e-6
NEG_INF = -1e30
FORCE_BONUS = 1e4

NSA_HEADS = 8
NSA_KV_GROUPS = 2
NSA_HPG = NSA_HEADS // NSA_KV_GROUPS
NSA_HEAD_DIM = 64
CMP_BLOCK = 32
CMP_STRIDE = 16
CMP_HIDDEN = 128
SEL_BLOCK = 64
SEL_TOPK = 16
N_LOCAL_FORCED = 2
WINDOW = 512
SEL_Q_CHUNK = 64
Q_BLOCK = 128

MLA_HEADS = 8
MLA_Q_LORA = 256
MLA_KV_LORA = 128
MLA_NOPE_DIM = 64
MLA_ROPE_DIM = 32
MLA_V_DIM = 64
ROPE_THETA = 10000.0

PEER_HEADS = 8
PEER_N_KEYS = 128
PEER_N_EXPERTS = PEER_N_KEYS * PEER_N_KEYS
PEER_QUERY_DIM = 256
PEER_TOPK = 16
PEER_TOKEN_CHUNK = 256

NSA_Q_W = NSA_HEADS * NSA_HEAD_DIM
NSA_KV_W = NSA_KV_GROUPS * NSA_HEAD_DIM
NSA_GATE_W = 3 * NSA_HEADS
MLA_OUT_W = MLA_HEADS * MLA_V_DIM
IN_SPLITS = (NSA_Q_W, NSA_KV_W, NSA_KV_W, NSA_KV_W, NSA_KV_W, NSA_KV_W, NSA_KV_W, NSA_GATE_W,
             MLA_Q_LORA, MLA_KV_LORA, MLA_ROPE_DIM, D_MODEL, D_MODEL)
IN_WIDTH = sum(IN_SPLITS)

kernel_name = 'nsa_mla_peer_hybrid_block'


def rms_norm(x, g):
    xf = x.astype(jnp.float32)
    y = xf * lax.rsqrt(jnp.mean(xf * xf, axis=-1, keepdims=True) + EPS)
    return (y * g.astype(jnp.float32)).astype(x.dtype)


def masked_softmax(s, mask):
    s = jnp.where(mask, s.astype(jnp.float32), NEG_INF)
    m = jnp.max(s, axis=-1, keepdims=True)
    p = jnp.where(mask, jnp.exp(s - m), 0.0)
    return p / jnp.maximum(jnp.sum(p, axis=-1, keepdims=True), 1e-30)


def alibi_slopes(n):
    return 2.0 ** (-8.0 * jnp.arange(1, n + 1, dtype=jnp.float32) / n)


def rope_cos_sin(positions, dim):
    inv = ROPE_THETA ** (-jnp.arange(0, dim, 2, dtype=jnp.float32) / dim)
    ang = positions.astype(jnp.float32)[..., None] * inv
    return jnp.cos(ang), jnp.sin(ang)


def apply_rope(x, cos, sin):
    x1, x2 = jnp.split(x.astype(jnp.float32), 2, axis=-1)
    return jnp.concatenate([x1 * cos - x2 * sin, x1 * sin + x2 * cos], axis=-1)


def _to_chunks(a, axis, n):
    shp = a.shape
    a = a.reshape(shp[:axis] + (n, shp[axis] // n) + shp[axis + 1:])
    return jnp.moveaxis(a, axis, 0)


def _from_chunks(a, axis):
    a = jnp.moveaxis(a, 0, axis)
    shp = a.shape
    return a.reshape(shp[:axis] + (shp[axis] * shp[axis + 1],) + shp[axis + 2:])


def _split_columns(a, sizes):
    offsets = np.cumsum(np.array(sizes))[:-1].tolist()
    return jnp.split(a, offsets, axis=-1)


def nsa_compress(kv_heads, pos_emb, w1, w2):
    T = kv_heads.shape[2]
    nc = (T - CMP_BLOCK) // CMP_STRIDE + 1
    idx = jnp.arange(nc)[:, None] * CMP_STRIDE + jnp.arange(CMP_BLOCK)[None, :]
    blocks = kv_heads[:, :, idx, :] + pos_emb.astype(jnp.float32)
    flat = blocks.reshape(blocks.shape[:3] + (CMP_BLOCK * NSA_HEAD_DIM,))
    return jax.nn.gelu(flat @ w1.astype(jnp.float32)) @ w2.astype(jnp.float32)


def nsa_attention(q, k_cmp, v_cmp, k_sel, v_sel, k_win, v_win, gate_logits,
                  cmp_pos_k, cmp_w1_k, cmp_w2_k, cmp_pos_v, cmp_w1_v, cmp_w2_v):
    f32 = jnp.float32
    B, T, _ = q.shape
    G, HPG, dk = NSA_KV_GROUPS, NSA_HPG, NSA_HEAD_DIM
    slopes = alibi_slopes(NSA_HEADS).reshape(G, HPG)
    pos = jnp.arange(T)
    qh = q.astype(f32).reshape(B, T, G, HPG, dk).transpose(0, 2, 3, 1, 4) * (dk ** -0.5)

    def heads(a):
        return a.astype(f32).reshape(B, T, G, dk).transpose(0, 2, 1, 3)

    kc = nsa_compress(heads(k_cmp), cmp_pos_k, cmp_w1_k, cmp_w2_k)
    vc = nsa_compress(heads(v_cmp), cmp_pos_v, cmp_w1_v, cmp_w2_v)
    nc = kc.shape[2]
    cmp_start = jnp.arange(nc) * CMP_STRIDE
    dist_c = pos[:, None] - (cmp_start + CMP_BLOCK - 1)[None, :]
    s_c = jnp.einsum('bghtd,bgnd->bghtn', qh, kc) - slopes[None, :, :, None, None] * dist_c
    p_c = masked_softmax(s_c, dist_c >= 0)
    o_cmp = jnp.einsum('bghtn,bgnd->bghtd', p_c, vc)

    ns = T // SEL_BLOCK
    sel_start = jnp.arange(ns) * SEL_BLOCK
    overlap = jnp.clip(jnp.minimum(cmp_start[:, None] + CMP_BLOCK, sel_start[None, :] + SEL_BLOCK)
                       - jnp.maximum(cmp_start[:, None], sel_start[None, :]), 0, None).astype(f32) / CMP_BLOCK
    importance = jnp.einsum('bghtn,ns->bgts', p_c, overlap)
    back = (pos // SEL_BLOCK)[:, None] - jnp.arange(ns)[None, :]
    causal_blk = back >= 0
    forced = causal_blk & ((back < N_LOCAL_FORCED) | (jnp.arange(ns)[None, :] == 0))
    sel_score = jnp.where(causal_blk, importance + FORCE_BONUS * forced.astype(f32), NEG_INF)
    top_s, sel_idx = lax.top_k(sel_score, min(SEL_TOPK, ns))
    sel_ok = top_s > 0.5 * NEG_INF
    ks_blocks = heads(k_sel).reshape(B, G, ns, SEL_BLOCK, dk)
    vs_blocks = heads(v_sel).reshape(B, G, ns, SEL_BLOCK, dk)
    bi = jnp.arange(B)[:, None, None, None]
    gi = jnp.arange(G)[None, :, None, None]

    def sel_chunk(args):
        qc, idx, ok, tq = args
        c, k = idx.shape[2], idx.shape[3]
        kg = ks_blocks[bi, gi, idx]
        vg = vs_blocks[bi, gi, idx]
        s = jnp.einsum('bghqd,bgqkld->bghqkl', qc, kg)
        kpos = idx[..., None] * SEL_BLOCK + jnp.arange(SEL_BLOCK)
        dist = tq[None, None, :, None, None] - kpos
        mask = ok[..., None] & (dist >= 0)
        s = s - slopes[None, :, :, None, None, None] * dist[:, :, None]
        p = masked_softmax(s.reshape(B, G, HPG, c, k * SEL_BLOCK), mask.reshape(B, G, 1, c, k * SEL_BLOCK))
        return jnp.einsum('bghqm,bgqmd->bghqd', p, vg.reshape(B, G, c, k * SEL_BLOCK, dk))

    nq = T // SEL_Q_CHUNK
    o_sel = _from_chunks(lax.map(sel_chunk, (_to_chunks(qh, 3, nq), _to_chunks(sel_idx, 2, nq),
                                             _to_chunks(sel_ok, 2, nq), pos.reshape(nq, SEL_Q_CHUNK))), 3)

    nb = T // Q_BLOCK
    kw_all = heads(k_win)
    vw_all = heads(v_win)
    band = jnp.arange(nb)[:, None] * Q_BLOCK - WINDOW + jnp.arange(WINDOW + Q_BLOCK)[None, :]

    def win_chunk(args):
        qc, kpos, tq = args
        kidx = jnp.clip(kpos, 0, T - 1)
        kb = kw_all[:, :, kidx]
        vb = vw_all[:, :, kidx]
        s = jnp.einsum('bghqd,bgkd->bghqk', qc, kb)
        dist = tq[:, None] - kpos[None, :]
        mask = (kpos[None, :] >= 0) & (dist >= 0) & (dist < WINDOW)
        s = s - slopes[None, :, :, None, None] * dist
        return jnp.einsum('bghqk,bgkd->bghqd', masked_softmax(s, mask), vb)

    o_win = _from_chunks(lax.map(win_chunk, (_to_chunks(qh, 3, nb), band, pos.reshape(nb, Q_BLOCK))), 3)

    gates = jax.nn.sigmoid(gate_logits.astype(f32)).reshape(B, T, 3, G, HPG).transpose(2, 0, 3, 4, 1)[..., None]
    o = gates[0] * o_cmp + gates[1] * o_sel + gates[2] * o_win
    return o.transpose(0, 3, 1, 2, 4).reshape(B, T, NSA_Q_W).astype(q.dtype)


def mla_attention(c_q, c_kv, k_rope, positions, q_norm_g, kv_norm_g, w_uq, w_ukv):
    f32 = jnp.float32
    B, T, _ = c_q.shape
    H = MLA_HEADS
    q = (rms_norm(c_q, q_norm_g) @ w_uq).astype(f32).reshape(B, T, H, MLA_NOPE_DIM + MLA_ROPE_DIM)
    kv = (rms_norm(c_kv, kv_norm_g) @ w_ukv).astype(f32).reshape(B, T, H, MLA_NOPE_DIM + MLA_V_DIM)
    q_nope, q_pe = q[..., :MLA_NOPE_DIM], q[..., MLA_NOPE_DIM:]
    k_nope, v = kv[..., :MLA_NOPE_DIM], kv[..., MLA_NOPE_DIM:]
    cos, sin = rope_cos_sin(positions, MLA_ROPE_DIM)
    q_pe = apply_rope(q_pe, cos[:, :, None], sin[:, :, None])
    k_pe = apply_rope(k_rope, cos, sin)
    qf = jnp.concatenate([q_nope, q_pe], axis=-1) * ((MLA_NOPE_DIM + MLA_ROPE_DIM) ** -0.5)
    kf = jnp.concatenate([k_nope, jnp.broadcast_to(k_pe[:, :, None, :], (B, T, H, MLA_ROPE_DIM))], axis=-1)
    qh = qf.transpose(0, 2, 1, 3)
    kh = kf.transpose(0, 2, 1, 3)
    vh = v.transpose(0, 2, 1, 3)
    pos = jnp.arange(T)
    nb = T // Q_BLOCK

    def blk(args):
        qc, tq = args
        s = jnp.einsum('bhqd,bhkd->bhqk', qc, kh)
        p = masked_softmax(s, tq[:, None] >= pos[None, :])
        return jnp.einsum('bhqk,bhkd->bhqd', p, vh)

    o = _from_chunks(lax.map(blk, (_to_chunks(qh, 2, nb), pos.reshape(nb, Q_BLOCK))), 2)
    return o.transpose(0, 2, 1, 3).reshape(B, T, MLA_OUT_W).astype(c_q.dtype)


def peer_ffn(x, w_query, sub_keys_1, sub_keys_2, expert_down, expert_up):
    f32 = jnp.float32
    B, T, D = x.shape
    H, K = PEER_HEADS, PEER_TOPK
    q = (x @ w_query).astype(f32).reshape(B, T, H, 2, PEER_QUERY_DIM // 2)
    s1 = jnp.einsum('bthd,nd->bthn', q[..., 0, :], sub_keys_1.astype(f32))
    s2 = jnp.einsum('bthd,nd->bthn', q[..., 1, :], sub_keys_2.astype(f32))
    v1, i1 = lax.top_k(s1, K)
    v2, i2 = lax.top_k(s2, K)
    cand_s = (v1[..., :, None] + v2[..., None, :]).reshape(B, T, H, K * K)
    cand_e = (i1[..., :, None] * PEER_N_KEYS + i2[..., None, :]).reshape(B, T, H, K * K)
    top_s, top_pos = lax.top_k(cand_s, K)
    experts = jnp.take_along_axis(cand_e, top_pos, axis=-1)
    g = jax.nn.softmax(top_s, axis=-1)
    n_tok = B * T
    c = math.gcd(n_tok, PEER_TOKEN_CHUNK)
    n_chunk = n_tok // c

    def expert_chunk(args):
        xt, e, gt = args
        u = expert_down[e].astype(f32)
        hdn = jax.nn.gelu(jnp.einsum('chkd,cd->chk', u, xt.astype(f32)))
        return jnp.einsum('chk,chkd->cd', gt * hdn, expert_up[e].astype(f32))

    y = lax.map(expert_chunk, (x.reshape(n_chunk, c, D), experts.reshape(n_chunk, c, H, K),
                               g.reshape(n_chunk, c, H, K)))
    return y.reshape(B, T, D).astype(x.dtype)


def setup_inputs(seed: int = 0) -> dict:
    key = jax.random.key(seed)
    ks = jax.random.split(key, 24)
    f32 = jnp.float32

    def nrm(k, shape, scale):
        return jax.random.normal(k, shape, f32) * scale

    def gain(k, n):
        return 1.0 + 0.02 * jax.random.normal(k, (DEPTH, n), f32)

    dk = NSA_HEAD_DIM
    return {
        'x': jax.random.normal(ks[0], (BATCH, SEQ, D_MODEL), f32),
        'positions': jnp.broadcast_to(jnp.arange(SEQ, dtype=jnp.int32)[None, :], (BATCH, SEQ)),
        'norm1_g': gain(ks[1], D_MODEL),
        'w_in': nrm(ks[2], (DEPTH, D_MODEL, IN_WIDTH), D_MODEL ** -0.5),
        'nsa_cmp_pos_k': nrm(ks[3], (DEPTH, CMP_BLOCK, dk), 0.1),
        'nsa_cmp_w1_k': nrm(ks[4], (DEPTH, CMP_BLOCK * dk, CMP_HIDDEN), (CMP_BLOCK * dk) ** -0.5),
        'nsa_cmp_w2_k': nrm(ks[5], (DEPTH, CMP_HIDDEN, dk), CMP_HIDDEN ** -0.5),
        'nsa_cmp_pos_v': nrm(ks[6], (DEPTH, CMP_BLOCK, dk), 0.1),
        'nsa_cmp_w1_v': nrm(ks[7], (DEPTH, CMP_BLOCK * dk, CMP_HIDDEN), (CMP_BLOCK * dk) ** -0.5),
        'nsa_cmp_w2_v': nrm(ks[8], (DEPTH, CMP_HIDDEN, dk), CMP_HIDDEN ** -0.5),
        'mla_q_norm_g': gain(ks[9], MLA_Q_LORA),
        'mla_kv_norm_g': gain(ks[10], MLA_KV_LORA),
        'mla_w_uq': nrm(ks[11], (DEPTH, MLA_Q_LORA, MLA_HEADS * (MLA_NOPE_DIM + MLA_ROPE_DIM)), MLA_Q_LORA ** -0.5),
        'mla_w_ukv': nrm(ks[12], (DEPTH, MLA_KV_LORA, MLA_HEADS * (MLA_NOPE_DIM + MLA_V_DIM)), MLA_KV_LORA ** -0.5),
        'w_branch_nsa': nrm(ks[13], (DEPTH, NSA_Q_W, D_MODEL), NSA_Q_W ** -0.5),
        'w_branch_mla': nrm(ks[14], (DEPTH, MLA_OUT_W, D_MODEL), MLA_OUT_W ** -0.5),
        'w_out': nrm(ks[15], (DEPTH, D_MODEL, D_MODEL), D_MODEL ** -0.5),
        'norm2_g': gain(ks[16], D_MODEL),
        'peer_w_query': nrm(ks[17], (DEPTH, D_MODEL, PEER_HEADS * PEER_QUERY_DIM), D_MODEL ** -0.5),
        'peer_sub_keys_1': nrm(ks[18], (DEPTH, PEER_N_KEYS, PEER_QUERY_DIM // 2), (PEER_QUERY_DIM // 2) ** -0.5),
        'peer_sub_keys_2': nrm(ks[19], (DEPTH, PEER_N_KEYS, PEER_QUERY_DIM // 2), (PEER_QUERY_DIM // 2) ** -0.5),
        'peer_expert_down': nrm(ks[20], (DEPTH, PEER_N_EXPERTS, D_MODEL), D_MODEL ** -0.5),
        'peer_expert_up': nrm(ks[21], (DEPTH, PEER_N_EXPERTS, D_MODEL), PEER_HEADS ** -0.5),
        'norm_f_g': 1.0 + 0.02 * jax.random.normal(ks[22], (D_MODEL,), f32),
    }


def reference(x, positions, norm1_g, w_in, nsa_cmp_pos_k, nsa_cmp_w1_k, nsa_cmp_w2_k,
              nsa_cmp_pos_v, nsa_cmp_w1_v, nsa_cmp_w2_v, mla_q_norm_g, mla_kv_norm_g,
              mla_w_uq, mla_w_ukv, w_branch_nsa, w_branch_mla, w_out, norm2_g,
              peer_w_query, peer_sub_keys_1, peer_sub_keys_2, peer_expert_down,
              peer_expert_up, norm_f_g):
    h = x
    for layer in range(DEPTH):
        xn = rms_norm(h, norm1_g[layer])
        (nsa_q, k_cmp, v_cmp, k_sel, v_sel, k_win, v_win, nsa_gates,
         mla_cq, mla_ckv, mla_kr, gate_nsa, gate_mla) = _split_columns(xn @ w_in[layer], IN_SPLITS)
        o_nsa = nsa_attention(nsa_q, k_cmp, v_cmp, k_sel, v_sel, k_win, v_win, nsa_gates,
                              nsa_cmp_pos_k[layer], nsa_cmp_w1_k[layer], nsa_cmp_w2_k[layer],
                              nsa_cmp_pos_v[layer], nsa_cmp_w1_v[layer], nsa_cmp_w2_v[layer])
        o_mla = mla_attention(mla_cq, mla_ckv, mla_kr, positions, mla_q_norm_g[layer],
                              mla_kv_norm_g[layer], mla_w_uq[layer], mla_w_ukv[layer])
        y_nsa = o_nsa @ w_branch_nsa[layer]
        y_mla = o_mla @ w_branch_mla[layer]
        merged = jax.nn.sigmoid(gate_nsa) * y_nsa + jax.nn.sigmoid(gate_mla) * y_mla
        h = h + merged @ w_out[layer]
        h = h + peer_ffn(rms_norm(h, norm2_g[layer]), peer_w_query[layer], peer_sub_keys_1[layer],
                         peer_sub_keys_2[layer], peer_expert_down[layer], peer_expert_up[layer])
    return rms_norm(h, norm_f_g)
```

```python
import functools
import math

import numpy as np
import jax
import jax.numpy as jnp
from jax import lax
from jax.experimental import pallas as pl
from jax.experimental.pallas import tpu as pltpu

F32 = jnp.float32
BF16 = jnp.bfloat16

D_MODEL = 1024
SEQ = 2048
EPS = 1e-6
NEG = -1e30
FORCE_BONUS = 1e4

NSA_HEADS = 8
NSA_GROUPS = 2
NSA_HPG = 4
HEAD_DIM = 64
CMP_BLOCK = 32
CMP_STRIDE = 16
CMP_HIDDEN = 128
N_CMP = SEQ // CMP_STRIDE
SEL_BLOCK = 64
N_SEL = SEQ // SEL_BLOCK
SEL_TOPK = 16
N_LOCAL_FORCED = 2
WINDOW = 512

MLA_HEADS = 8
MLA_Q_LORA = 256
MLA_KV_LORA = 128
MLA_NOPE = 64
MLA_ROPE = 32
MLA_V = 64
ROPE_THETA = 10000.0

PEER_HEADS = 8
PEER_KEYS = 128
PEER_EXPERTS = PEER_KEYS * PEER_KEYS
PEER_QDIM = 256
PEER_TOPK = 16

LANE = 128
SEL_LANE0 = 64
POS_HI_LANE = 96
POS_LO_LANE = 97

VMEM_LIMIT = 56 * 1024 * 1024

IN_SPLITS = (512, 128, 128, 128, 128, 128, 128, 24, 256, 128, 32, 1024, 1024)


def _cparams(sem):
    return pltpu.CompilerParams(dimension_semantics=sem, vmem_limit_bytes=VMEM_LIMIT)


def _dot(a, b):
    return jnp.dot(a, b, preferred_element_type=F32)


def _dot_nt(a, b):
    return lax.dot_general(a, b, (((1,), (1,)), ((), ())), preferred_element_type=F32)


def _rms(x, g):
    return x * lax.rsqrt(jnp.mean(x * x, axis=-1, keepdims=True) + EPS) * g


def _const_spec(shape):
    nd = len(shape)
    return pl.BlockSpec(shape, lambda *_: (0,) * nd)


_SEG = dict(q=1024, kc=128, vc=128, ks=256, vs=256, kw=256, vw=256, g3=1536,
            cq=256, ckv=128, kr=128, krr=128, gn=1024, gm=1024)
_SEG_OFF = {}
_o = 0
for _k, _w in _SEG.items():
    _SEG_OFF[_k] = (_o, _o + _w)
    _o += _w
W_PAD = _o


def _pad_heads(w, n_heads, width):
    k = w.shape[0]
    return jnp.pad(w.reshape(k, n_heads, width), ((0, 0), (0, 0), (0, LANE - width))).reshape(k, n_heads * LANE)


def _dup_heads(w, n_heads, width):
    k = w.shape[0]
    w3 = w.reshape(k, n_heads, width)
    return jnp.concatenate([w3, w3], axis=-1).reshape(k, n_heads * 2 * width)


def _rope_rot_cols(w_pe):
    half = MLA_ROPE // 2
    return jnp.concatenate([-w_pe[..., half:], w_pe[..., :half]], axis=-1)


def _build_w_in(w_in):
    offs = np.cumsum((0,) + IN_SPLITS)
    seg = [w_in[:, offs[i]:offs[i + 1]] for i in range(len(IN_SPLITS))]
    wq, wkc, wvc, wks, wvs, wkw, wvw, wg, wcq, wckv, wkr, wgn, wgm = seg
    k = w_in.shape[0]
    g3 = jnp.broadcast_to(wg.reshape(k, 3 * NSA_HEADS, 1), (k, 3 * NSA_HEADS, HEAD_DIM)).reshape(k, 3 * 512)
    kr = jnp.pad(wkr, ((0, 0), (SEL_LANE0, LANE - SEL_LANE0 - MLA_ROPE)))
    krr = jnp.pad(_rope_rot_cols(wkr), ((0, 0), (SEL_LANE0, LANE - SEL_LANE0 - MLA_ROPE)))
    cols = [_pad_heads(wq * (HEAD_DIM ** -0.5), NSA_HEADS, HEAD_DIM), wkc, wvc,
            _pad_heads(wks, NSA_GROUPS, HEAD_DIM), _dup_heads(wvs, NSA_GROUPS, HEAD_DIM),
            _pad_heads(wkw, NSA_GROUPS, HEAD_DIM), _dup_heads(wvw, NSA_GROUPS, HEAD_DIM),
            g3, wcq, wckv, kr, krr, wgn, wgm]
    return jnp.concatenate(cols, axis=1).astype(BF16)


def _inproj_kernel(x_ref, pos_ref, g1_ref, w_ref, qg_ref, kvg_ref, wuq_ref, wuqr_ref, wk_ref, wv_ref,
                   invl_ref, qconst_ref,
                   q_out, kc_out, vc_out, ks_out, vs_out, kw_out, vw_out, g3_out,
                   qm_out, km_out, vm_out, gn_out, gm_out, *, tm):
    xn = _rms(x_ref[...], g1_ref[...]).astype(BF16)

    def proj(name, c0=0, c1=None):
        a, b = _SEG_OFF[name]
        c1 = b - a if c1 is None else c1
        return _dot(xn, w_ref[:, a + c0:a + c1])

    for h in range(NSA_HEADS):
        sl = slice(h * LANE, (h + 1) * LANE)
        q_out[:, sl] = (proj('q', h * LANE, (h + 1) * LANE) + qconst_ref[:, sl]).astype(BF16)
    kc_out[...] = proj('kc').astype(BF16)
    vc_out[...] = proj('vc').astype(BF16)

    row = lax.broadcasted_iota(jnp.int32, (tm, LANE), 0)
    lane = lax.broadcasted_iota(jnp.int32, (tm, LANE), 1)
    t = (pl.program_id(0) * tm + row) & (SEQ - 1)
    blk = t >> 6
    pos_aug = (jnp.where(lane == POS_HI_LANE, blk, 0) + jnp.where(lane == POS_LO_LANE, t & 63, 0)).astype(F32)
    sel_aug = pos_aug + jnp.where(lane == SEL_LANE0 + blk, 1.0, 0.0)
    for g in range(NSA_GROUPS):
        sl = slice(g * LANE, (g + 1) * LANE)
        ks_out[:, sl] = (proj('ks', g * LANE, (g + 1) * LANE) + sel_aug).astype(BF16)
        kw_out[:, sl] = (proj('kw', g * LANE, (g + 1) * LANE) + pos_aug).astype(BF16)
    vs_out[...] = proj('vs').astype(BF16)
    vw_out[...] = proj('vw').astype(BF16)
    for c in range(3):
        g3_out[:, c * 512:(c + 1) * 512] = jax.nn.sigmoid(proj('g3', c * 512, (c + 1) * 512))
    for c in range(2):
        gn_out[:, c * 512:(c + 1) * 512] = jax.nn.sigmoid(proj('gn', c * 512, (c + 1) * 512))
        gm_out[:, c * 512:(c + 1) * 512] = jax.nn.sigmoid(proj('gm', c * 512, (c + 1) * 512))

    ang = pos_ref[...].astype(F32) * invl_ref[...]
    cos, sin = jnp.cos(ang), jnp.sin(ang)
    cqn = _rms(proj('cq'), qg_ref[...]).astype(BF16)
    ckvn = _rms(proj('ckv'), kvg_ref[...]).astype(BF16)
    kpe = proj('kr') * cos + proj('krr') * sin
    qscale = (MLA_NOPE + MLA_ROPE) ** -0.5
    for h in range(MLA_HEADS):
        sl = slice(h * LANE, (h + 1) * LANE)
        qh = _dot(cqn, wuq_ref[:, sl]) * cos + _dot(cqn, wuqr_ref[:, sl]) * sin
        qm_out[:, sl] = (qh * qscale).astype(BF16)
        km_out[:, sl] = (_dot(ckvn, wk_ref[:, sl]) + kpe).astype(BF16)
    vm_out[...] = _dot(ckvn, wv_ref[...]).astype(BF16)


def _inproj(x2, pos2, norm1_g, w_in, q_norm_g, kv_norm_g, w_uq, w_ukv, *, tm=256):
    n = x2.shape[0]
    w_pad = _build_w_in(w_in)
    uq3 = w_uq.reshape(MLA_Q_LORA, MLA_HEADS, MLA_NOPE + MLA_ROPE)
    wuq = jnp.pad(uq3, ((0, 0), (0, 0), (0, LANE - MLA_NOPE - MLA_ROPE))).reshape(MLA_Q_LORA, -1).astype(BF16)
    uq_rot = jnp.concatenate([jnp.zeros_like(uq3[..., :MLA_NOPE]), _rope_rot_cols(uq3[..., MLA_NOPE:])], axis=-1)
    wuqr = jnp.pad(uq_rot, ((0, 0), (0, 0), (0, LANE - MLA_NOPE - MLA_ROPE))).reshape(MLA_Q_LORA, -1).astype(BF16)
    ukv3 = w_ukv.reshape(MLA_KV_LORA, MLA_HEADS, MLA_NOPE + MLA_V)
    wk = jnp.pad(ukv3[..., :MLA_NOPE], ((0, 0), (0, 0), (0, LANE - MLA_NOPE))).reshape(MLA_KV_LORA, -1).astype(BF16)
    wv = ukv3[..., MLA_NOPE:].reshape(MLA_KV_LORA, -1).astype(BF16)

    inv = ROPE_THETA ** (-np.arange(0, MLA_ROPE, 2, dtype=np.float32) / MLA_ROPE)
    invl = np.zeros((1, LANE), np.float32)
    invl[0, MLA_NOPE:MLA_NOPE + MLA_ROPE] = np.concatenate([inv, inv])
    slopes = 2.0 ** (-8.0 * np.arange(1, NSA_HEADS + 1, dtype=np.float32) / NSA_HEADS)
    qconst = np.zeros((1, NSA_HEADS, LANE), np.float32)
    qconst[0, :, POS_HI_LANE] = slopes * SEL_BLOCK
    qconst[0, :, POS_LO_LANE] = slopes
    qconst = qconst.reshape(1, NSA_HEADS * LANE)

    row = lambda w: pl.BlockSpec((tm, w), lambda i: (i, 0))
    outs = [('q', 1024, BF16), ('kc', 128, BF16), ('vc', 128, BF16), ('ks', 256, BF16), ('vs', 256, BF16),
            ('kw', 256, BF16), ('vw', 256, BF16), ('g3', 1536, F32), ('qm', 1024, BF16), ('km', 1024, BF16),
            ('vm', 512, BF16), ('gn', 1024, F32), ('gm', 1024, F32)]
    res = pl.pallas_call(
        functools.partial(_inproj_kernel, tm=tm),
        grid=(n // tm,),
        in_specs=[row(D_MODEL), row(1), _const_spec((1, D_MODEL)), _const_spec((D_MODEL, W_PAD)),
                  _const_spec((1, MLA_Q_LORA)), _const_spec((1, MLA_KV_LORA)),
                  _const_spec(wuq.shape), _const_spec(wuqr.shape), _const_spec(wk.shape), _const_spec(wv.shape),
                  _const_spec((1, LANE)), _const_spec((1, NSA_HEADS * LANE))],
        out_specs=[row(w) for _, w, _ in outs],
        out_shape=[jax.ShapeDtypeStruct((n, w), dt) for _, w, dt in outs],
        compiler_params=_cparams(("parallel",)),
    )(x2, pos2, norm1_g.reshape(1, -1), w_pad, q_norm_g.reshape(1, -1), kv_norm_g.reshape(1, -1),
      wuq, wuqr, wk, wv, jnp.asarray(invl), jnp.asarray(qconst))
    return dict(zip([o[0] for o in outs], res))


def _compress_kernel(zk_ref, zv_ref, pk_ref, pv_ref, w1k_ref, w1v_ref, w2k_ref, w2v_ref, aug_ref,
                     kc_out, vc_out):
    half = CMP_STRIDE * NSA_GROUPS * HEAD_DIM

    def mlp(z_ref, p_ref, w1_ref, w2_ref):
        z = z_ref[0]
        top = _dot(z, w1_ref[0:half, :])
        bot = _dot(z, w1_ref[half:2 * half, :])
        pos = _dot(p_ref[:, 0:half], w1_ref[0:half, :]) + _dot(p_ref[:, half:2 * half], w1_ref[half:2 * half, :])
        pre = top + pltpu.roll(bot, N_CMP - 1, 0) + pos[0:1, :]
        return _dot(jax.nn.gelu(pre).astype(BF16), w2_ref[...])

    kc_out[0] = (mlp(zk_ref, pk_ref, w1k_ref, w2k_ref) + aug_ref[...]).astype(BF16)
    vc_out[0] = mlp(zv_ref, pv_ref, w1v_ref, w2v_ref).astype(BF16)


def _expand_w1(w1):
    w = w1.reshape(2, CMP_STRIDE, HEAD_DIM, CMP_HIDDEN)
    eye = jnp.eye(NSA_GROUPS, dtype=w1.dtype)
    w = jnp.einsum('hldc,ge->hlgdec', w, eye)
    return w.reshape(2 * CMP_STRIDE * NSA_GROUPS * HEAD_DIM, NSA_GROUPS * CMP_HIDDEN).astype(BF16)


def _expand_pos(pos):
    p = jnp.broadcast_to(pos.reshape(2, CMP_STRIDE, 1, HEAD_DIM), (2, CMP_STRIDE, NSA_GROUPS, HEAD_DIM))
    return jnp.broadcast_to(p.reshape(1, -1), (8, 2 * CMP_STRIDE * NSA_GROUPS * HEAD_DIM)).astype(BF16)


def _expand_w2(w2, dup):
    blk = jnp.concatenate([w2, w2 if dup else jnp.zeros_like(w2)], axis=-1)
    eye = jnp.eye(NSA_GROUPS, dtype=w2.dtype)
    return jnp.einsum('cd,ge->gced', blk, eye).reshape(NSA_GROUPS * CMP_HIDDEN, NSA_GROUPS * LANE).astype(BF16)


def _compress(kc_raw, vc_raw, pos_k, w1_k, w2_k, pos_v, w1_v, w2_v, batch):
    zk = kc_raw.reshape(batch, N_CMP, CMP_STRIDE * LANE)
    zv = vc_raw.reshape(batch, N_CMP, CMP_STRIDE * LANE)
    endpos = np.arange(N_CMP) * CMP_STRIDE + CMP_BLOCK - 1
    aug = np.zeros((N_CMP, NSA_GROUPS, LANE), np.float32)
    aug[:, :, POS_HI_LANE] = (endpos // SEL_BLOCK)[:, None]
    aug[:, :, POS_LO_LANE] = (endpos % SEL_BLOCK)[:, None]
    aug = aug.reshape(N_CMP, NSA_GROUPS * LANE)
    zspec = pl.BlockSpec((1, N_CMP, CMP_STRIDE * LANE), lambda b: (b, 0, 0))
    ospec = pl.BlockSpec((1, N_CMP, NSA_GROUPS * LANE), lambda b: (b, 0, 0))
    args = (zk, zv, _expand_pos(pos_k), _expand_pos(pos_v), _expand_w1(w1_k), _expand_w1(w1_v),
            _expand_w2(w2_k, False), _expand_w2(w2_v, True), jnp.asarray(aug))
    return pl.pallas_call(
        _compress_kernel,
        grid=(batch,),
        in_specs=[zspec, zspec] + [_const_spec(a.shape) for a in args[2:]],
        out_specs=[ospec, ospec],
        out_shape=[jax.ShapeDtypeStruct((batch, N_CMP, NSA_GROUPS * LANE), BF16)] * 2,
        compiler_params=_cparams(("parallel",)),
    )(*args)


def _nsa_cmp_kernel(q_ref, kc_ref, vc_ref, gate_ref, ov_ref, ocmp_out, selb_out, *, tq):
    qi = pl.program_id(1)
    lane = lax.broadcasted_iota(jnp.int32, (tq, LANE), 1)
    tpos = qi * tq + lax.broadcasted_iota(jnp.int32, (tq, LANE), 0)
    visible = tpos >= lane * CMP_STRIDE + (CMP_BLOCK - 1)
    sblk = lane - SEL_LANE0
    back = (tpos >> 6) - sblk
    in_sel = (lane >= SEL_LANE0) & (lane < SEL_LANE0 + N_SEL)
    causal = in_sel & (back >= 0)
    forced = causal & ((back < N_LOCAL_FORCED) | (sblk == 0))
    low = lane < HEAD_DIM

    for g in range(NSA_GROUPS):
        kc = kc_ref[0, :, g * LANE:(g + 1) * LANE]
        vc = vc_ref[0, :, g * LANE:(g + 1) * LANE]
        imp = jnp.zeros((tq, LANE), F32)
        outs = []
        for hh in range(NSA_HPG):
            h = g * NSA_HPG + hh
            s = jnp.where(visible, _dot_nt(q_ref[:, h * LANE:(h + 1) * LANE], kc), NEG)
            m = jnp.max(s, axis=-1, keepdims=True)
            p = jnp.where(visible, jnp.exp(s - m), 0.0)
            p = p / jnp.maximum(jnp.sum(p, axis=-1, keepdims=True), 1e-30)
            pb = p.astype(BF16)
            outs.append(_dot(pb, vc))
            imp = imp + _dot(pb, ov_ref[...])
        for pr in range(NSA_HPG // 2):
            c0 = (g * 2 + pr) * LANE
            o = jnp.where(low, outs[2 * pr], outs[2 * pr + 1])
            ocmp_out[:, c0:c0 + LANE] = o * gate_ref[:, c0:c0 + LANE]

        score = jnp.where(causal, imp + jnp.where(forced, FORCE_BONUS, 0.0), NEG)
        rank = jnp.zeros((tq, LANE), jnp.int32)
        for s2 in range(N_SEL):
            col = jnp.broadcast_to(score[:, SEL_LANE0 + s2:SEL_LANE0 + s2 + 1], (tq, LANE))
            ahead = (col > score) | ((col == score) & (sblk > s2))
            rank = rank + ahead.astype(jnp.int32)
        chosen = causal & (rank < SEL_TOPK)
        selb_out[:, g * LANE:(g + 1) * LANE] = jnp.where(in_sel & ~chosen, NEG, 0.0).astype(BF16)


def _nsa_cmp(q, kc, vc, g3, batch, *, tq=256):
    n = q.shape[0]
    nq = SEQ // tq
    cmp_start = np.arange(N_CMP) * CMP_STRIDE
    sel_start = np.arange(N_SEL) * SEL_BLOCK
    ov = np.clip(np.minimum(cmp_start[:, None] + CMP_BLOCK, sel_start[None, :] + SEL_BLOCK)
                 - np.maximum(cmp_start[:, None], sel_start[None, :]), 0, None).astype(np.float32) / CMP_BLOCK
    ov_placed = np.zeros((N_CMP, LANE), np.float32)
    ov_placed[:, SEL_LANE0:SEL_LANE0 + N_SEL] = ov
    rows = lambda w: pl.BlockSpec((tq, w), lambda b, i: (b * nq + i, 0))
    cspec = pl.BlockSpec((1, N_CMP, NSA_GROUPS * LANE), lambda b, i: (b, 0, 0))
    return pl.pallas_call(
        functools.partial(_nsa_cmp_kernel, tq=tq),
        grid=(batch, nq),
        in_specs=[rows(1024), cspec, cspec, rows(512), _const_spec((N_CMP, LANE))],
        out_specs=[rows(512), rows(NSA_GROUPS * LANE)],
        out_shape=[jax.ShapeDtypeStruct((n, 512), F32), jax.ShapeDtypeStruct((n, NSA_GROUPS * LANE), BF16)],
        compiler_params=_cparams(("parallel", "parallel")),
    )(q, kc, vc, g3, jnp.asarray(ov_placed, BF16))


def _flash_init(m_ref, l_ref, acc_ref):
    m_ref[...] = jnp.full(m_ref.shape, NEG, F32)
    l_ref[...] = jnp.zeros(l_ref.shape, F32)
    acc_ref[...] = jnp.zeros(acc_ref.shape, F32)


def _flash_update(s, v, m_ref, l_ref, acc_ref):
    m_old = m_ref[...]
    m_new = jnp.maximum(m_old, jnp.max(s, axis=-1, keepdims=True))
    alpha = jnp.exp(m_old - m_new)
    p = jnp.exp(s - m_new)
    l_ref[...] = alpha * l_ref[...] + jnp.sum(p, axis=-1, keepdims=True)
    acc_ref[...] = alpha * acc_ref[...] + _dot(p.astype(BF16), v)
    m_ref[...] = m_new


def _tile_iotas(heads, t):
    row = lax.broadcasted_iota(jnp.int32, (heads * t, t), 0) & (t - 1)
    col = lax.broadcasted_iota(jnp.int32, (heads * t, t), 1)
    return row, col


def _nsa_attn_kernel(q_ref, selb_ref, ks_ref, vs_ref, kw_ref, vw_ref, gs_ref, gw_ref, ocmp_ref, o_out,
                     m_ref, l_ref, acc_ref, *, t):
    qi = pl.program_id(2)
    row, col = _tile_iotas(NSA_HPG, t)
    qs = [q_ref[:, hh * LANE:(hh + 1) * LANE] for hh in range(NSA_HPG)]
    q_win = jnp.concatenate(qs, axis=0)
    q_sel = jnp.concatenate([q + selb_ref[...] for q in qs], axis=0)
    low = lax.broadcasted_iota(jnp.int32, (t, LANE), 1) < HEAD_DIM

    def kv(ref, j):
        return ref[pl.ds(pl.multiple_of(j * t, t), t), :]

    def finish():
        o = acc_ref[...] / l_ref[...]
        return [jnp.where(low, o[(2 * p) * t:(2 * p + 1) * t], o[(2 * p + 1) * t:(2 * p + 2) * t])
                for p in range(NSA_HPG // 2)]

    _flash_init(m_ref, l_ref, acc_ref)
    s = _dot_nt(q_sel, kv(ks_ref, qi))
    _flash_update(jnp.where(col <= row, s, NEG), kv(vs_ref, qi), m_ref, l_ref, acc_ref)

    def sel_body(j, carry):
        _flash_update(_dot_nt(q_sel, kv(ks_ref, j)), kv(vs_ref, j), m_ref, l_ref, acc_ref)
        return carry

    lax.fori_loop(0, qi, sel_body, 0)
    o_sel = finish()

    _flash_init(m_ref, l_ref, acc_ref)
    s = _dot_nt(q_win, kv(kw_ref, qi))
    _flash_update(jnp.where(col <= row, s, NEG), kv(vw_ref, qi), m_ref, l_ref, acc_ref)

    @pl.when(qi >= 1)
    def _():
        _flash_update(_dot_nt(q_win, kv(kw_ref, qi - 1)), kv(vw_ref, qi - 1), m_ref, l_ref, acc_ref)

    @pl.when(qi >= 2)
    def _():
        s = _dot_nt(q_win, kv(kw_ref, qi - 2))
        _flash_update(jnp.where(col > row, s, NEG), kv(vw_ref, qi - 2), m_ref, l_ref, acc_ref)

    o_win = finish()
    for p in range(NSA_HPG // 2):
        sl = slice(p * LANE, (p + 1) * LANE)
        o_out[:, sl] = (gs_ref[:, sl] * o_sel[p] + gw_ref[:, sl] * o_win[p] + ocmp_ref[:, sl]).astype(BF16)


def _nsa_attn(q, selb, ks, vs, kw, vw, g3, ocmp, batch, *, t=256):
    assert WINDOW == 2 * t
    n = q.shape[0]
    nq = SEQ // t
    gw = NSA_HPG * HEAD_DIM
    rows = lambda w, off=0: pl.BlockSpec((t, w), lambda b, g, i: (b * nq + i, g + off))
    seq = pl.BlockSpec((SEQ, LANE), lambda b, g, i: (b, g))
    m = NSA_HPG * t
    return pl.pallas_call(
        functools.partial(_nsa_attn_kernel, t=t),
        grid=(batch, NSA_GROUPS, nq),
        in_specs=[rows(NSA_HPG * LANE), rows(LANE), seq, seq, seq, seq,
                  rows(gw, 2), rows(gw, 4), rows(gw)],
        out_specs=rows(gw),
        out_shape=jax.ShapeDtypeStruct((n, NSA_HEADS * HEAD_DIM), BF16),
        scratch_shapes=[pltpu.VMEM((m, 1), F32), pltpu.VMEM((m, 1), F32), pltpu.VMEM((m, LANE), F32)],
        compiler_params=_cparams(("parallel", "parallel", "arbitrary")),
    )(q, selb, ks, vs, kw, vw, g3, g3, ocmp)


def _mla_attn_kernel(q_ref, k_ref, v_ref, o_out, m_ref, l_ref, acc_ref, *, t):
    qi = pl.program_id(2)
    row, col = _tile_iotas(2, t)
    q0, q1 = q_ref[:, 0:LANE], q_ref[:, LANE:2 * LANE]

    def scores(j):
        k = k_ref[pl.ds(pl.multiple_of(j * t, t), t), :]
        return jnp.concatenate([_dot_nt(q0, k[:, 0:LANE]), _dot_nt(q1, k[:, LANE:2 * LANE])], axis=0)

    def v(j):
        return v_ref[pl.ds(pl.multiple_of(j * t, t), t), :]

    _flash_init(m_ref, l_ref, acc_ref)
    _flash_update(jnp.where(col <= row, scores(qi), NEG), v(qi), m_ref, l_ref, acc_ref)

    def body(j, carry):
        _flash_update(scores(j), v(j), m_ref, l_ref, acc_ref)
        return carry

    lax.fori_loop(0, qi, body, 0)
    o = acc_ref[...] / l_ref[...]
    low = lax.broadcasted_iota(jnp.int32, (t, LANE), 1) < MLA_V
    o_out[...] = jnp.where(low, o[0:t], o[t:2 * t]).astype(BF16)


def _mla_attn(qm, km, vm, batch, *, t=256):
    n = qm.shape[0]
    nq = SEQ // t
    return pl.pallas_call(
        functools.partial(_mla_attn_kernel, t=t),
        grid=(batch, MLA_HEADS // 2, nq),
        in_specs=[pl.BlockSpec((t, 2 * LANE), lambda b, p, i: (b * nq + i, p)),
                  pl.BlockSpec((SEQ, 2 * LANE), lambda b, p, i: (b, p)),
                  pl.BlockSpec((SEQ, LANE), lambda b, p, i: (b, p))],
        out_specs=pl.BlockSpec((t, LANE), lambda b, p, i: (b * nq + i, p)),
        out_shape=jax.ShapeDtypeStruct((n, MLA_HEADS * MLA_V), BF16),
        scratch_shapes=[pltpu.VMEM((2 * t, 1), F32), pltpu.VMEM((2 * t, 1), F32), pltpu.VMEM((2 * t, LANE), F32)],
        compiler_params=_cparams(("parallel", "parallel", "arbitrary")),
    )(qm, km, vm)


def _mix_kernel(x_ref, on_ref, om_ref, gn_ref, gm_ref, wn_ref, wm_ref, wo_ref, g2_ref, wq_ref, k1_ref, k2_ref,
                h_out, xn_out, s_out):
    merged = gn_ref[...] * _dot(on_ref[...], wn_ref[...]) + gm_ref[...] * _dot(om_ref[...], wm_ref[...])
    h = x_ref[...] + _dot(merged.astype(BF16), wo_ref[...])
    h_out[...] = h
    xn = _rms(h, g2_ref[...]).astype(BF16)
    xn_out[...] = xn
    half = PEER_QDIM // 2
    for hd in range(PEER_HEADS):
        q = _dot(xn, wq_ref[:, hd * PEER_QDIM:(hd + 1) * PEER_QDIM]).astype(BF16)
        s_out[2 * hd] = _dot_nt(k1_ref[...], q[:, 0:half])
        s_out[2 * hd + 1] = _dot_nt(k2_ref[...], q[:, half:PEER_QDIM])


def _mix(x2, o_nsa, o_mla, gn, gm, w_n, w_m, w_o, norm2_g, w_q, keys1, keys2, *, tm=256):
    n = x2.shape[0]
    row = lambda w: pl.BlockSpec((tm, w), lambda i: (i, 0))
    ws = [w_n.astype(BF16), w_m.astype(BF16), w_o.astype(BF16), norm2_g.reshape(1, -1), w_q.astype(BF16),
          keys1.astype(BF16), keys2.astype(BF16)]
    return pl.pallas_call(
        _mix_kernel,
        grid=(n // tm,),
        in_specs=[row(D_MODEL), row(512), row(512), row(D_MODEL), row(D_MODEL)] + [_const_spec(w.shape) for w in ws],
        out_specs=[row(D_MODEL), row(D_MODEL), pl.BlockSpec((2 * PEER_HEADS, PEER_KEYS, tm), lambda i: (0, 0, i))],
        out_shape=[jax.ShapeDtypeStruct((n, D_MODEL), F32), jax.ShapeDtypeStruct((n, D_MODEL), BF16),
                   jax.ShapeDtypeStruct((2 * PEER_HEADS, PEER_KEYS, n), F32)],
        compiler_params=_cparams(("parallel",)),
    )(x2, o_nsa, o_mla, gn, gm, *ws)


_CAND_ROWS = 16 + 7 * 8 + 8


def _extract_top(v, k):
    n, tl = v.shape
    idx = lax.broadcasted_iota(jnp.int32, (n, tl), 0)
    rank = jnp.full((n, tl), k, jnp.int32)
    vals = []
    for r in range(k):
        m = jnp.max(v, axis=0, keepdims=True)
        first = jnp.min(jnp.where(v == m, idx, n), axis=0, keepdims=True)
        hit = idx == first
        rank = jnp.where(hit, r, rank)
        v = jnp.where(hit, -jnp.inf, v)
        vals.append(m)
    return rank, vals


def _peer_topk_kernel(s_ref, cnt_out, ea_out, r2_out, eb_out):
    k = PEER_TOPK
    s1, s2 = s_ref[0], s_ref[1]
    tl = s1.shape[1]
    rank1, a = _extract_top(s1, k)
    rank2, b = _extract_top(s2, k)
    bcat = jnp.concatenate(b, axis=0)
    groups = [a[0] + bcat]
    for p in range(1, 8):
        groups.append(a[p] + bcat[0:8])
    groups.append(jnp.concatenate(a[8:16], axis=0) + b[0])
    cand = jnp.concatenate(groups, axis=0)
    ridx = lax.broadcasted_iota(jnp.int32, (_CAND_ROWS, tl), 0)
    qpos = jnp.where(ridx < 16, ridx, (ridx - 16) & 7)
    ppos = jnp.where(ridx < 16, 0, ((ridx - 16) >> 3) + 1)
    valid = (ridx >= 16 + 7 * 8) | ((ppos + 1) * (qpos + 1) <= k)
    top = cand[0:1]
    work = jnp.where(valid, cand, -jnp.inf)
    taken = jnp.zeros((_CAND_ROWS, tl), jnp.bool_)
    for _ in range(k):
        m = jnp.max(work, axis=0, keepdims=True)
        first = jnp.min(jnp.where(work == m, ridx, _CAND_ROWS), axis=0, keepdims=True)
        hit = ridx == first
        taken = taken | hit
        work = jnp.where(hit, -jnp.inf, work)
    takenf = taken.astype(F32)
    z = jnp.sum(jnp.where(taken, jnp.exp(cand - top), 0.0), axis=0, keepdims=True)
    cnt = jnp.zeros((PEER_KEYS, tl), F32)
    for p in range(k):
        if p == 0:
            c = jnp.sum(takenf[0:16], axis=0, keepdims=True)
        elif p < 8:
            c = jnp.sum(takenf[16 + (p - 1) * 8:16 + p * 8], axis=0, keepdims=True)
        else:
            c = takenf[72 + (p - 8):72 + (p - 8) + 1]
        cnt = jnp.where(rank1 == p, c, cnt)
    cnt_out[0] = cnt
    ea_out[0] = jnp.where(rank1 < k, jnp.exp(s1 - a[0]), 0.0) / z
    r2_out[0] = rank2.astype(F32).astype(BF16)
    eb_out[0] = jnp.where(rank2 < k, jnp.exp(s2 - b[0]), 0.0).astype(BF16)


def _peer_topk(scores, *, tl=256):
    n = scores.shape[-1]
    ospec = pl.BlockSpec((1, PEER_KEYS, tl), lambda i, h: (h, 0, i))
    shp = (PEER_HEADS, PEER_KEYS, n)
    return pl.pallas_call(
        _peer_topk_kernel,
        grid=(n // tl, PEER_HEADS),
        in_specs=[pl.BlockSpec((2, PEER_KEYS, tl), lambda i, h: (h, 0, i))],
        out_specs=[ospec] * 4,
        out_shape=[jax.ShapeDtypeStruct(shp, F32), jax.ShapeDtypeStruct(shp, F32),
                   jax.ShapeDtypeStruct(shp, BF16), jax.ShapeDtypeStruct(shp, BF16)],
        compiler_params=_cparams(("parallel", "parallel")),
    )(scores)


def _peer_dense_kernel(xn_ref, d_ref, ut_ref, cnt_ref, ea_ref, r2_ref, eb_ref, y_out, g_ref, *, rows_per_step):
    e = pl.program_id(1)

    @pl.when(e == 0)
    def _():
        y_out[...] = jnp.zeros(y_out.shape, F32)

    hid = _dot_nt(d_ref[...], xn_ref[...])
    for ii in range(rows_per_step):
        sl = slice(ii * PEER_KEYS, (ii + 1) * PEER_KEYS)
        w = None
        for hd in range(PEER_HEADS):
            cnt = cnt_ref[hd, ii:ii + 1, :].astype(BF16)
            ea = ea_ref[hd, ii:ii + 1, :].astype(BF16)
            wh = jnp.where(r2_ref[hd] < cnt, eb_ref[hd] * ea, jnp.zeros((), BF16))
            w = wh if w is None else w + wh
        g_ref[sl, :] = jax.nn.gelu(hid[sl, :]).astype(BF16) * w
    y_out[...] += _dot(ut_ref[...], g_ref[...])


def _peer_dense(xn, d_bf, ut_bf, cnt, ea, r2, eb, *, tl=512, eb_rows=8):
    n = xn.shape[0]
    ebk = eb_rows * PEER_KEYS
    full = pl.BlockSpec((PEER_HEADS, PEER_KEYS, tl), lambda i, e: (0, 0, i))
    part = pl.BlockSpec((PEER_HEADS, eb_rows, tl), lambda i, e: (0, e, i))
    return pl.pallas_call(
        functools.partial(_peer_dense_kernel, rows_per_step=eb_rows),
        grid=(n // tl, PEER_EXPERTS // ebk),
        in_specs=[pl.BlockSpec((tl, D_MODEL), lambda i, e: (i, 0)),
                  pl.BlockSpec((ebk, D_MODEL), lambda i, e: (e, 0)),
                  pl.BlockSpec((D_MODEL, ebk), lambda i, e: (0, e)),
                  part, part, full, full],
        out_specs=pl.BlockSpec((D_MODEL, tl), lambda i, e: (0, i)),
        out_shape=jax.ShapeDtypeStruct((D_MODEL, n), F32),
        scratch_shapes=[pltpu.VMEM((ebk, tl), BF16)],
        compiler_params=_cparams(("parallel", "arbitrary")),
    )(xn, d_bf, ut_bf, cnt, ea, r2, eb)


def _final_kernel(h_ref, yt_ref, g_ref, o_out):
    o_out[...] = _rms(h_ref[...] + yt_ref[...].T, g_ref[...])


def _final(h, yt, norm_f_g, *, tm=256):
    n = h.shape[0]
    return pl.pallas_call(
        _final_kernel,
        grid=(n // tm,),
        in_specs=[pl.BlockSpec((tm, D_MODEL), lambda i: (i, 0)), pl.BlockSpec((D_MODEL, tm), lambda i: (0, i)),
                  _const_spec((1, D_MODEL))],
        out_specs=pl.BlockSpec((tm, D_MODEL), lambda i: (i, 0)),
        out_shape=jax.ShapeDtypeStruct((n, D_MODEL), F32),
        compiler_params=_cparams(("parallel",)),
    )(h, yt, norm_f_g.reshape(1, -1))


def _layer(h2, pos2, batch, p):
    a = _inproj(h2, pos2, p['norm1_g'], p['w_in'], p['mla_q_norm_g'], p['mla_kv_norm_g'], p['mla_w_uq'],
                p['mla_w_ukv'])
    kc, vc = _compress(a['kc'], a['vc'], p['nsa_cmp_pos_k'], p['nsa_cmp_w1_k'], p['nsa_cmp_w2_k'],
                       p['nsa_cmp_pos_v'], p['nsa_cmp_w1_v'], p['nsa_cmp_w2_v'], batch)
    ocmp, selb = _nsa_cmp(a['q'], kc, vc, a['g3'], batch)
    o_nsa = _nsa_attn(a['q'], selb, a['ks'], a['vs'], a['kw'], a['vw'], a['g3'], ocmp, batch)
    o_mla = _mla_attn(a['qm'], a['km'], a['vm'], batch)
    h_mid, xn, scores = _mix(h2, o_nsa, o_mla, a['gn'], a['gm'], p['w_branch_nsa'], p['w_branch_mla'], p['w_out'],
                             p['norm2_g'], p['peer_w_query'], p['peer_sub_keys_1'], p['peer_sub_keys_2'])
    cnt, ea, r2, eb = _peer_topk(scores)
    yt = _peer_dense(xn, p['peer_expert_down'].astype(BF16), p['peer_expert_up'].T.astype(BF16), cnt, ea, r2, eb)
    return h_mid, yt


def kernel(x, positions, norm1_g, w_in, nsa_cmp_pos_k, nsa_cmp_w1_k, nsa_cmp_w2_k, nsa_cmp_pos_v, nsa_cmp_w1_v,
           nsa_cmp_w2_v, mla_q_norm_g, mla_kv_norm_g, mla_w_uq, mla_w_ukv, w_branch_nsa, w_branch_mla, w_out,
           norm2_g, peer_w_query, peer_sub_keys_1, peer_sub_keys_2, peer_expert_down, peer_expert_up, norm_f_g):
    batch, seq, d = x.shape
    assert seq == SEQ and d == D_MODEL
    stacked = dict(norm1_g=norm1_g, w_in=w_in, nsa_cmp_pos_k=nsa_cmp_pos_k, nsa_cmp_w1_k=nsa_cmp_w1_k,
                   nsa_cmp_w2_k=nsa_cmp_w2_k, nsa_cmp_pos_v=nsa_cmp_pos_v, nsa_cmp_w1_v=nsa_cmp_w1_v,
                   nsa_cmp_w2_v=nsa_cmp_w2_v, mla_q_norm_g=mla_q_norm_g, mla_kv_norm_g=mla_kv_norm_g,
                   mla_w_uq=mla_w_uq, mla_w_ukv=mla_w_ukv, w_branch_nsa=w_branch_nsa, w_branch_mla=w_branch_mla,
                   w_out=w_out, norm2_g=norm2_g, peer_w_query=peer_w_query, peer_sub_keys_1=peer_sub_keys_1,
                   peer_sub_keys_2=peer_sub_keys_2, peer_expert_down=peer_expert_down,
                   peer_expert_up=peer_expert_up)
    assert w_in.shape[0] == 1, "single-layer block"
    p = {k: v[0] for k, v in stacked.items()}
    h_mid, yt = _layer(x.reshape(batch * seq, d), positions.reshape(batch * seq, 1), batch, p)
    return _final(h_mid, yt, norm_f_g).reshape(batch, seq, d)
```

```python
import functools
import math

import numpy as np
import jax
import jax.numpy as jnp
from jax import lax
from jax.experimental import pallas as pl
from jax.experimental.pallas import tpu as pltpu

F32 = jnp.float32
BF16 = jnp.bfloat16

D_MODEL = 1024
SEQ = 2048
EPS = 1e-6
NEG = -1e30
FORCE_BONUS = 1e4

NSA_HEADS = 8
NSA_GROUPS = 2
NSA_HPG = 4
HEAD_DIM = 64
CMP_BLOCK = 32
CMP_STRIDE = 16
CMP_HIDDEN = 128
N_CMP = SEQ // CMP_STRIDE
SEL_BLOCK = 64
N_SEL = SEQ // SEL_BLOCK
SEL_TOPK = 16
N_LOCAL_FORCED = 2
WINDOW = 512

MLA_HEADS = 8
MLA_Q_LORA = 256
MLA_KV_LORA = 128
MLA_NOPE = 64
MLA_ROPE = 32
MLA_V = 64
ROPE_THETA = 10000.0

PEER_HEADS = 8
PEER_KEYS = 128
PEER_EXPERTS = PEER_KEYS * PEER_KEYS
PEER_QDIM = 256
PEER_TOPK = 16

LANE = 128
SEL_LANE0 = 64
POS_HI_LANE = 96
POS_LO_LANE = 97

VMEM_LIMIT = 56 * 1024 * 1024

IN_SPLITS = (512, 128, 128, 128, 128, 128, 128, 24, 256, 128, 32, 1024, 1024)


def _cparams(sem):
    return pltpu.CompilerParams(dimension_semantics=sem, vmem_limit_bytes=VMEM_LIMIT)


def _dot(a, b):
    return jnp.dot(a, b, preferred_element_type=F32)


def _dot_nt(a, b):
    return lax.dot_general(a, b, (((1,), (1,)), ((), ())), preferred_element_type=F32)


def _rms(x, g):
    return x * lax.rsqrt(jnp.mean(x * x, axis=-1, keepdims=True) + EPS) * g


_GELU_K1 = -2.0 * math.sqrt(2.0 / math.pi) * math.log2(math.e)
_GELU_K2 = _GELU_K1 * 0.044715


def _gelu_tanh(x):
    return x / (1.0 + jnp.exp2(x * (_GELU_K1 + _GELU_K2 * (x * x))))


def _const_spec(shape):
    nd = len(shape)
    return pl.BlockSpec(shape, lambda *_: (0,) * nd)


_SEG = dict(q=1024, kc=128, vc=128, ks=256, kw=256, g3=1536,
            cq=256, ckv=128, kr=128, krr=128, gn=1024, gm=1024)
_SEG_OFF = {}
_o = 0
for _k, _w in _SEG.items():
    _SEG_OFF[_k] = (_o, _o + _w)
    _o += _w
W_PAD = _o


def _pad_heads(w, n_heads, width):
    k = w.shape[0]
    return jnp.pad(w.reshape(k, n_heads, width), ((0, 0), (0, 0), (0, LANE - width))).reshape(k, n_heads * LANE)


def _dup_heads(w, n_heads, width):
    k = w.shape[0]
    w3 = w.reshape(k, n_heads, width)
    return jnp.concatenate([w3, w3], axis=-1).reshape(k, n_heads * 2 * width)


def _rope_rot_cols(w_pe):
    half = MLA_ROPE // 2
    return jnp.concatenate([-w_pe[..., half:], w_pe[..., :half]], axis=-1)


def _build_w_in(w_in):
    offs = np.cumsum((0,) + IN_SPLITS)
    seg = [w_in[:, offs[i]:offs[i + 1]] for i in range(len(IN_SPLITS))]
    wq, wkc, wvc, wks, wvs, wkw, wvw, wg, wcq, wckv, wkr, wgn, wgm = seg
    k = w_in.shape[0]
    g3 = jnp.broadcast_to(wg.reshape(k, 3 * NSA_HEADS, 1), (k, 3 * NSA_HEADS, HEAD_DIM)).reshape(k, 3 * 512)
    kr = jnp.pad(wkr, ((0, 0), (SEL_LANE0, LANE - SEL_LANE0 - MLA_ROPE)))
    krr = jnp.pad(_rope_rot_cols(wkr), ((0, 0), (SEL_LANE0, LANE - SEL_LANE0 - MLA_ROPE)))
    cols = [_pad_heads(wq * (HEAD_DIM ** -0.5), NSA_HEADS, HEAD_DIM), wkc, wvc,
            _pad_heads(wks, NSA_GROUPS, HEAD_DIM), _pad_heads(wkw, NSA_GROUPS, HEAD_DIM),
            g3, wcq, wckv, kr, krr, wgn, wgm]
    w_vt = jnp.concatenate([_dup_heads(wvs, NSA_GROUPS, HEAD_DIM), _dup_heads(wvw, NSA_GROUPS, HEAD_DIM)], axis=1).T
    return jnp.concatenate(cols, axis=1).astype(BF16), w_vt.astype(BF16)


def _inproj_kernel(x_ref, pos_ref, g1_ref, w_ref, wvt_ref, qg_ref, kvg_ref, wuq_ref, wuqr_ref, wk_ref, wv_ref,
                   invl_ref, qconst_ref,
                   q_out, kc_out, vc_out, ks_out, vs_out, kw_out, vw_out, g3_out,
                   qm_out, km_out, vm_out, gn_out, gm_out, *, tm):
    xn = _rms(x_ref[...], g1_ref[...]).astype(BF16)

    def proj(name, c0=0, c1=None):
        a, b = _SEG_OFF[name]
        c1 = b - a if c1 is None else c1
        return _dot(xn, w_ref[:, a + c0:a + c1])

    for h in range(NSA_HEADS):
        sl = slice(h * LANE, (h + 1) * LANE)
        q_out[:, sl] = (proj('q', h * LANE, (h + 1) * LANE) + qconst_ref[:, sl]).astype(BF16)
    kc_out[...] = proj('kc').astype(BF16)
    vc_out[...] = proj('vc').astype(BF16)

    row = lax.broadcasted_iota(jnp.int32, (tm, LANE), 0)
    lane = lax.broadcasted_iota(jnp.int32, (tm, LANE), 1)
    t = (pl.program_id(0) * tm + row) & (SEQ - 1)
    blk = t >> 6
    pos_aug = (jnp.where(lane == POS_HI_LANE, blk, 0) + jnp.where(lane == POS_LO_LANE, t & 63, 0)).astype(F32)
    sel_aug = pos_aug + jnp.where(lane == SEL_LANE0 + blk, 1.0, 0.0)
    for g in range(NSA_GROUPS):
        sl = slice(g * LANE, (g + 1) * LANE)
        ks_out[:, sl] = (proj('ks', g * LANE, (g + 1) * LANE) + sel_aug).astype(BF16)
        kw_out[:, sl] = (proj('kw', g * LANE, (g + 1) * LANE) + pos_aug).astype(BF16)
    vs_out[0] = _dot_nt(wvt_ref[0:2 * LANE, :], xn).astype(BF16)
    vw_out[0] = _dot_nt(wvt_ref[2 * LANE:4 * LANE, :], xn).astype(BF16)
    for c in range(3):
        g3_out[:, c * 512:(c + 1) * 512] = jax.nn.sigmoid(proj('g3', c * 512, (c + 1) * 512))
    for c in range(2):
        gn_out[:, c * 512:(c + 1) * 512] = jax.nn.sigmoid(proj('gn', c * 512, (c + 1) * 512))
        gm_out[:, c * 512:(c + 1) * 512] = jax.nn.sigmoid(proj('gm', c * 512, (c + 1) * 512))

    ang = pos_ref[...].astype(F32) * invl_ref[...]
    cos, sin = jnp.cos(ang), jnp.sin(ang)
    cqn = _rms(proj('cq'), qg_ref[...]).astype(BF16)
    ckvn = _rms(proj('ckv'), kvg_ref[...]).astype(BF16)
    kpe = proj('kr') * cos + proj('krr') * sin
    qscale = (MLA_NOPE + MLA_ROPE) ** -0.5
    for h in range(MLA_HEADS):
        sl = slice(h * LANE, (h + 1) * LANE)
        qh = _dot(cqn, wuq_ref[:, sl]) * cos + _dot(cqn, wuqr_ref[:, sl]) * sin
        qm_out[:, sl] = (qh * qscale).astype(BF16)
        km_out[:, sl] = (_dot(ckvn, wk_ref[:, sl]) + kpe).astype(BF16)
    vm_out[0] = _dot_nt(wv_ref[...], ckvn).astype(BF16)


def _inproj(x2, pos2, norm1_g, w_in, q_norm_g, kv_norm_g, w_uq, w_ukv, *, tm=256):
    n = x2.shape[0]
    w_pad, w_vt = _build_w_in(w_in)
    uq3 = w_uq.reshape(MLA_Q_LORA, MLA_HEADS, MLA_NOPE + MLA_ROPE)
    wuq = jnp.pad(uq3, ((0, 0), (0, 0), (0, LANE - MLA_NOPE - MLA_ROPE))).reshape(MLA_Q_LORA, -1).astype(BF16)
    uq_rot = jnp.concatenate([jnp.zeros_like(uq3[..., :MLA_NOPE]), _rope_rot_cols(uq3[..., MLA_NOPE:])], axis=-1)
    wuqr = jnp.pad(uq_rot, ((0, 0), (0, 0), (0, LANE - MLA_NOPE - MLA_ROPE))).reshape(MLA_Q_LORA, -1).astype(BF16)
    ukv3 = w_ukv.reshape(MLA_KV_LORA, MLA_HEADS, MLA_NOPE + MLA_V)
    wk = jnp.pad(ukv3[..., :MLA_NOPE], ((0, 0), (0, 0), (0, LANE - MLA_NOPE))).reshape(MLA_KV_LORA, -1).astype(BF16)
    wv = ukv3[..., MLA_NOPE:].reshape(MLA_KV_LORA, -1).T.astype(BF16)

    inv = ROPE_THETA ** (-np.arange(0, MLA_ROPE, 2, dtype=np.float32) / MLA_ROPE)
    invl = np.zeros((1, LANE), np.float32)
    invl[0, MLA_NOPE:MLA_NOPE + MLA_ROPE] = np.concatenate([inv, inv])
    slopes = 2.0 ** (-8.0 * np.arange(1, NSA_HEADS + 1, dtype=np.float32) / NSA_HEADS)
    qconst = np.zeros((1, NSA_HEADS, LANE), np.float32)
    qconst[0, :, POS_HI_LANE] = slopes * SEL_BLOCK
    qconst[0, :, POS_LO_LANE] = slopes
    qconst = qconst.reshape(1, NSA_HEADS * LANE)

    row = lambda w: pl.BlockSpec((tm, w), lambda i: (i, 0))
    outs = [('q', 1024, BF16), ('kc', 128, BF16), ('vc', 128, BF16), ('ks', 256, BF16), ('vs', -256, BF16),
            ('kw', 256, BF16), ('vw', -256, BF16), ('g3', 1536, F32), ('qm', 1024, BF16), ('km', 1024, BF16),
            ('vm', -512, BF16), ('gn', 1024, F32), ('gm', 1024, F32)]
    ospec = lambda w: row(w) if w > 0 else pl.BlockSpec((1, -w, tm), lambda i: (i, 0, 0))
    oshape = lambda w: (n, w) if w > 0 else (n // tm, -w, tm)
    res = pl.pallas_call(
        functools.partial(_inproj_kernel, tm=tm),
        grid=(n // tm,),
        in_specs=[row(D_MODEL), row(1), _const_spec((1, D_MODEL)), _const_spec((D_MODEL, W_PAD)),
                  _const_spec(w_vt.shape), _const_spec((1, MLA_Q_LORA)), _const_spec((1, MLA_KV_LORA)),
                  _const_spec(wuq.shape), _const_spec(wuqr.shape), _const_spec(wk.shape), _const_spec(wv.shape),
                  _const_spec((1, LANE)), _const_spec((1, NSA_HEADS * LANE))],
        out_specs=[ospec(w) for _, w, _ in outs],
        out_shape=[jax.ShapeDtypeStruct(oshape(w), dt) for _, w, dt in outs],
        compiler_params=_cparams(("parallel",)),
    )(x2, pos2, norm1_g.reshape(1, -1), w_pad, w_vt, q_norm_g.reshape(1, -1), kv_norm_g.reshape(1, -1),
      wuq, wuqr, wk, wv, jnp.asarray(invl), jnp.asarray(qconst))
    return dict(zip([o[0] for o in outs], res))


def _compress_kernel(zk_ref, zv_ref, pk_ref, pv_ref, w1k_ref, w1v_ref, w2k_ref, w2vt_ref, aug_ref,
                     kc_out, vc_out):
    half = CMP_STRIDE * NSA_GROUPS * HEAD_DIM

    def hidden(z_ref, p_ref, w1_ref):
        z = z_ref[0]
        top = _dot(z, w1_ref[0:half, :])
        bot = _dot(z, w1_ref[half:2 * half, :])
        pos = _dot(p_ref[:, 0:half], w1_ref[0:half, :]) + _dot(p_ref[:, half:2 * half], w1_ref[half:2 * half, :])
        pre = top + pltpu.roll(bot, N_CMP - 1, 0) + pos[0:1, :]
        return jax.nn.gelu(pre).astype(BF16)

    kc_out[0] = (_dot(hidden(zk_ref, pk_ref, w1k_ref), w2k_ref[...]) + aug_ref[...]).astype(BF16)
    vc_out[0] = _dot_nt(w2vt_ref[...], hidden(zv_ref, pv_ref, w1v_ref)).astype(BF16)


def _expand_w1(w1):
    w = w1.reshape(2, CMP_STRIDE, HEAD_DIM, CMP_HIDDEN)
    eye = jnp.eye(NSA_GROUPS, dtype=w1.dtype)
    w = jnp.einsum('hldc,ge->hlgdec', w, eye)
    return w.reshape(2 * CMP_STRIDE * NSA_GROUPS * HEAD_DIM, NSA_GROUPS * CMP_HIDDEN).astype(BF16)


def _expand_pos(pos):
    p = jnp.broadcast_to(pos.reshape(2, CMP_STRIDE, 1, HEAD_DIM), (2, CMP_STRIDE, NSA_GROUPS, HEAD_DIM))
    return jnp.broadcast_to(p.reshape(1, -1), (8, 2 * CMP_STRIDE * NSA_GROUPS * HEAD_DIM)).astype(BF16)


def _expand_w2(w2, dup):
    blk = jnp.concatenate([w2, w2 if dup else jnp.zeros_like(w2)], axis=-1)
    eye = jnp.eye(NSA_GROUPS, dtype=w2.dtype)
    return jnp.einsum('cd,ge->gced', blk, eye).reshape(NSA_GROUPS * CMP_HIDDEN, NSA_GROUPS * LANE).astype(BF16)


def _compress(kc_raw, vc_raw, pos_k, w1_k, w2_k, pos_v, w1_v, w2_v, batch):
    zk = kc_raw.reshape(batch, N_CMP, CMP_STRIDE * LANE)
    zv = vc_raw.reshape(batch, N_CMP, CMP_STRIDE * LANE)
    endpos = np.arange(N_CMP) * CMP_STRIDE + CMP_BLOCK - 1
    aug = np.zeros((N_CMP, NSA_GROUPS, LANE), np.float32)
    aug[:, :, POS_HI_LANE] = (endpos // SEL_BLOCK)[:, None]
    aug[:, :, POS_LO_LANE] = (endpos % SEL_BLOCK)[:, None]
    aug = aug.reshape(N_CMP, NSA_GROUPS * LANE)
    zspec = pl.BlockSpec((1, N_CMP, CMP_STRIDE * LANE), lambda b: (b, 0, 0))
    ospec = pl.BlockSpec((1, N_CMP, NSA_GROUPS * LANE), lambda b: (b, 0, 0))
    args = (zk, zv, _expand_pos(pos_k), _expand_pos(pos_v), _expand_w1(w1_k), _expand_w1(w1_v),
            _expand_w2(w2_k, False), _expand_w2(w2_v, True).T, jnp.asarray(aug))
    return pl.pallas_call(
        _compress_kernel,
        grid=(batch,),
        in_specs=[zspec, zspec] + [_const_spec(a.shape) for a in args[2:]],
        out_specs=[ospec, pl.BlockSpec((1, NSA_GROUPS * LANE, N_CMP), lambda b: (b, 0, 0))],
        out_shape=[jax.ShapeDtypeStruct((batch, N_CMP, NSA_GROUPS * LANE), BF16),
                   jax.ShapeDtypeStruct((batch, NSA_GROUPS * LANE, N_CMP), BF16)],
        compiler_params=_cparams(("parallel",)),
    )(*args)


def _nsa_cmp_kernel(q_ref, kc_ref, vct_ref, gate_ref, ovt_ref, ocmp_out, selb_out, *, tq):
    qi = pl.program_id(1)
    slot = lax.broadcasted_iota(jnp.int32, (N_CMP, tq), 0)
    tslot = qi * tq + lax.broadcasted_iota(jnp.int32, (N_CMP, tq), 1)
    visible = tslot >= slot * CMP_STRIDE + (CMP_BLOCK - 1)
    sblk = lax.broadcasted_iota(jnp.int32, (N_SEL, tq), 0)
    back = ((qi * tq + lax.broadcasted_iota(jnp.int32, (N_SEL, tq), 1)) >> 6) - sblk
    causal = back >= 0
    forced = causal & ((back < N_LOCAL_FORCED) | (sblk == 0))
    low_rows = lax.broadcasted_iota(jnp.int32, (LANE, tq), 0) < HEAD_DIM

    for g in range(NSA_GROUPS):
        kc = kc_ref[0, :, g * LANE:(g + 1) * LANE]
        vct = vct_ref[0, g * LANE:(g + 1) * LANE, :]
        imp = jnp.zeros((N_SEL, tq), F32)
        outs = []
        for hh in range(NSA_HPG):
            h = g * NSA_HPG + hh
            s = jnp.where(visible, _dot_nt(kc, q_ref[:, h * LANE:(h + 1) * LANE]), NEG)
            m = jnp.max(s, axis=0, keepdims=True)
            p = jnp.where(visible, jnp.exp(s - m), 0.0)
            p = p / jnp.maximum(jnp.sum(p, axis=0, keepdims=True), 1e-30)
            pb = p.astype(BF16)
            outs.append(_dot(vct, pb))
            imp = imp + _dot(ovt_ref[...], pb)
        for pr in range(NSA_HPG // 2):
            c0 = (g * 2 + pr) * LANE
            o = jnp.where(low_rows, outs[2 * pr], outs[2 * pr + 1]).T
            ocmp_out[:, c0:c0 + LANE] = o * gate_ref[:, c0:c0 + LANE]

        score = jnp.where(causal, imp + jnp.where(forced, FORCE_BONUS, 0.0), NEG)
        rank = jnp.zeros((N_SEL, tq), jnp.int32)
        for s2 in range(N_SEL):
            other = score[s2:s2 + 1]
            ahead = (other > score) | ((other == score) & (sblk > s2))
            rank = rank + ahead.astype(jnp.int32)
        bias = jnp.where(causal & (rank < SEL_TOPK), 0.0, NEG)
        placed = jnp.concatenate([jnp.zeros((SEL_LANE0, tq), F32), bias,
                                  jnp.zeros((LANE - SEL_LANE0 - N_SEL, tq), F32)], axis=0)
        selb_out[:, g * LANE:(g + 1) * LANE] = placed.T.astype(BF16)


def _nsa_cmp(q, kc, vct, g3, batch, *, tq=256):
    n = q.shape[0]
    nq = SEQ // tq
    cmp_start = np.arange(N_CMP) * CMP_STRIDE
    sel_start = np.arange(N_SEL) * SEL_BLOCK
    ov = np.clip(np.minimum(cmp_start[:, None] + CMP_BLOCK, sel_start[None, :] + SEL_BLOCK)
                 - np.maximum(cmp_start[:, None], sel_start[None, :]), 0, None).astype(np.float32) / CMP_BLOCK
    rows = lambda w: pl.BlockSpec((tq, w), lambda b, i: (b * nq + i, 0))
    return pl.pallas_call(
        functools.partial(_nsa_cmp_kernel, tq=tq),
        grid=(batch, nq),
        in_specs=[rows(1024), pl.BlockSpec((1, N_CMP, NSA_GROUPS * LANE), lambda b, i: (b, 0, 0)),
                  pl.BlockSpec((1, NSA_GROUPS * LANE, N_CMP), lambda b, i: (b, 0, 0)), rows(512),
                  _const_spec((N_SEL, N_CMP))],
        out_specs=[rows(512), rows(NSA_GROUPS * LANE)],
        out_shape=[jax.ShapeDtypeStruct((n, 512), F32), jax.ShapeDtypeStruct((n, NSA_GROUPS * LANE), BF16)],
        compiler_params=_cparams(("parallel", "parallel")),
    )(q, kc, vct, g3, jnp.asarray(ov.T, BF16))


def _flash_init(m_ref, l_ref, acc_ref):
    m_ref[...] = jnp.full(m_ref.shape, NEG, F32)
    l_ref[...] = jnp.zeros(l_ref.shape, F32)
    acc_ref[...] = jnp.zeros(acc_ref.shape, F32)


def _flash_update(st, vt, m_ref, l_ref, acc_ref):
    m_old = m_ref[...]
    m_new = jnp.maximum(m_old, jnp.max(st, axis=0, keepdims=True))
    alpha = jnp.exp(m_old - m_new)
    p = jnp.exp(st - m_new)
    l_ref[...] = alpha * l_ref[...] + jnp.sum(p, axis=0, keepdims=True)
    acc_ref[...] = alpha * acc_ref[...] + _dot(vt, p.astype(BF16))
    m_ref[...] = m_new


def _tile_iotas(heads, t):
    key = lax.broadcasted_iota(jnp.int32, (t, heads * t), 0)
    qry = lax.broadcasted_iota(jnp.int32, (t, heads * t), 1) & (t - 1)
    return key, qry


def _nsa_attn_kernel(q_ref, selb_ref, ks_ref, vs_ref, kw_ref, vw_ref, gs_ref, gw_ref, ocmp_ref, o_out,
                     m_ref, l_ref, acc_ref, *, t):
    qi = pl.program_id(2)
    key, qry = _tile_iotas(NSA_HPG, t)
    qs = [q_ref[:, hh * LANE:(hh + 1) * LANE] for hh in range(NSA_HPG)]
    q_win = jnp.concatenate(qs, axis=0)
    q_sel = jnp.concatenate([q + selb_ref[...] for q in qs], axis=0)
    low_rows = lax.broadcasted_iota(jnp.int32, (LANE, t), 0) < HEAD_DIM

    def keys(ref, j):
        return ref[pl.ds(pl.multiple_of(j * t, t), t), :]

    def finish():
        o = acc_ref[...] / l_ref[...]
        return [jnp.where(low_rows, o[:, (2 * p) * t:(2 * p + 1) * t], o[:, (2 * p + 1) * t:(2 * p + 2) * t]).T
                for p in range(NSA_HPG // 2)]

    _flash_init(m_ref, l_ref, acc_ref)
    s = _dot_nt(keys(ks_ref, qi), q_sel)
    _flash_update(jnp.where(key <= qry, s, NEG), vs_ref[qi], m_ref, l_ref, acc_ref)

    def sel_body(j, carry):
        _flash_update(_dot_nt(keys(ks_ref, j), q_sel), vs_ref[j], m_ref, l_ref, acc_ref)
        return carry

    lax.fori_loop(0, qi, sel_body, 0)
    o_sel = finish()

    _flash_init(m_ref, l_ref, acc_ref)
    s = _dot_nt(keys(kw_ref, qi), q_win)
    _flash_update(jnp.where(key <= qry, s, NEG), vw_ref[qi], m_ref, l_ref, acc_ref)

    @pl.when(qi >= 1)
    def _():
        _flash_update(_dot_nt(keys(kw_ref, qi - 1), q_win), vw_ref[qi - 1], m_ref, l_ref, acc_ref)

    @pl.when(qi >= 2)
    def _():
        s = _dot_nt(keys(kw_ref, qi - 2), q_win)
        _flash_update(jnp.where(key > qry, s, NEG), vw_ref[qi - 2], m_ref, l_ref, acc_ref)

    o_win = finish()
    for p in range(NSA_HPG // 2):
        sl = slice(p * LANE, (p + 1) * LANE)
        o_out[:, sl] = (gs_ref[:, sl] * o_sel[p] + gw_ref[:, sl] * o_win[p] + ocmp_ref[:, sl]).astype(BF16)


def _nsa_attn(q, selb, ks, vs, kw, vw, g3, ocmp, batch, *, t=256):
    assert WINDOW == 2 * t
    n = q.shape[0]
    nq = SEQ // t
    gw = NSA_HPG * HEAD_DIM
    rows = lambda w, off=0: pl.BlockSpec((t, w), lambda b, g, i: (b * nq + i, g + off))
    seq = pl.BlockSpec((SEQ, LANE), lambda b, g, i: (b, g))
    seq_t = pl.BlockSpec((nq, LANE, t), lambda b, g, i: (b, g, 0))
    m = NSA_HPG * t
    return pl.pallas_call(
        functools.partial(_nsa_attn_kernel, t=t),
        grid=(batch, NSA_GROUPS, nq),
        in_specs=[rows(NSA_HPG * LANE), rows(LANE), seq, seq_t, seq, seq_t,
                  rows(gw, 2), rows(gw, 4), rows(gw)],
        out_specs=rows(gw),
        out_shape=jax.ShapeDtypeStruct((n, NSA_HEADS * HEAD_DIM), BF16),
        scratch_shapes=[pltpu.VMEM((1, m), F32), pltpu.VMEM((1, m), F32), pltpu.VMEM((LANE, m), F32)],
        compiler_params=_cparams(("parallel", "parallel", "arbitrary")),
    )(q, selb, ks, vs, kw, vw, g3, g3, ocmp)


def _mla_attn_kernel(q_ref, k_ref, v_ref, o_out, m_ref, l_ref, acc_ref, *, tq, tk):
    qi = pl.program_id(2)
    per = tq // tk
    key = lax.broadcasted_iota(jnp.int32, (tk, 2 * tq), 0)
    qry = lax.broadcasted_iota(jnp.int32, (tk, 2 * tq), 1) & (tq - 1)
    q0, q1 = q_ref[:, 0:LANE], q_ref[:, LANE:2 * LANE]

    def scores(j):
        k = k_ref[pl.ds(pl.multiple_of(j * tk, tk), tk), :]
        return jnp.concatenate([_dot_nt(k[:, 0:LANE], q0), _dot_nt(k[:, LANE:2 * LANE], q1)], axis=1)

    _flash_init(m_ref, l_ref, acc_ref)
    for d in range(per):
        j = qi * per + d
        _flash_update(jnp.where(key + d * tk <= qry, scores(j), NEG), v_ref[j], m_ref, l_ref, acc_ref)

    def body(j, carry):
        _flash_update(scores(j), v_ref[j], m_ref, l_ref, acc_ref)
        return carry

    lax.fori_loop(0, qi * per, body, 0)
    o = acc_ref[...] / l_ref[...]
    low_rows = lax.broadcasted_iota(jnp.int32, (LANE, tq), 0) < MLA_V
    o_out[...] = jnp.where(low_rows, o[:, 0:tq], o[:, tq:2 * tq]).T.astype(BF16)


def _mla_attn(qm, km, vm, batch, *, tq=512):
    n = qm.shape[0]
    nk, _, tk = vm.shape
    nq = SEQ // tq
    return pl.pallas_call(
        functools.partial(_mla_attn_kernel, tq=tq, tk=tk),
        grid=(batch, MLA_HEADS // 2, nq),
        in_specs=[pl.BlockSpec((tq, 2 * LANE), lambda b, p, i: (b * nq + i, p)),
                  pl.BlockSpec((SEQ, 2 * LANE), lambda b, p, i: (b, p)),
                  pl.BlockSpec((SEQ // tk, LANE, tk), lambda b, p, i: (b, p, 0))],
        out_specs=pl.BlockSpec((tq, LANE), lambda b, p, i: (b * nq + i, p)),
        out_shape=jax.ShapeDtypeStruct((n, MLA_HEADS * MLA_V), BF16),
        scratch_shapes=[pltpu.VMEM((1, 2 * tq), F32), pltpu.VMEM((1, 2 * tq), F32),
                        pltpu.VMEM((LANE, 2 * tq), F32)],
        compiler_params=_cparams(("parallel", "parallel", "arbitrary")),
    )(qm, km, vm)


def _mix_kernel(x_ref, on_ref, om_ref, gn_ref, gm_ref, wn_ref, wm_ref, wo_ref, g2_ref, wq_ref, k1_ref, k2_ref,
                h_out, xn_out, s_out):
    merged = gn_ref[...] * _dot(on_ref[...], wn_ref[...]) + gm_ref[...] * _dot(om_ref[...], wm_ref[...])
    h = x_ref[...] + _dot(merged.astype(BF16), wo_ref[...])
    h_out[...] = h
    xn = _rms(h, g2_ref[...]).astype(BF16)
    xn_out[...] = xn
    half = PEER_QDIM // 2
    for hd in range(PEER_HEADS):
        q = _dot(xn, wq_ref[:, hd * PEER_QDIM:(hd + 1) * PEER_QDIM]).astype(BF16)
        s_out[2 * hd] = _dot_nt(k1_ref[...], q[:, 0:half])
        s_out[2 * hd + 1] = _dot_nt(k2_ref[...], q[:, half:PEER_QDIM])


def _mix(x2, o_nsa, o_mla, gn, gm, w_n, w_m, w_o, norm2_g, w_q, keys1, keys2, *, tm=256):
    n = x2.shape[0]
    row = lambda w: pl.BlockSpec((tm, w), lambda i: (i, 0))
    ws = [w_n.astype(BF16), w_m.astype(BF16), w_o.astype(BF16), norm2_g.reshape(1, -1), w_q.astype(BF16),
          keys1.astype(BF16), keys2.astype(BF16)]
    return pl.pallas_call(
        _mix_kernel,
        grid=(n // tm,),
        in_specs=[row(D_MODEL), row(512), row(512), row(D_MODEL), row(D_MODEL)] + [_const_spec(w.shape) for w in ws],
        out_specs=[row(D_MODEL), row(D_MODEL), pl.BlockSpec((2 * PEER_HEADS, PEER_KEYS, tm), lambda i: (0, 0, i))],
        out_shape=[jax.ShapeDtypeStruct((n, D_MODEL), F32), jax.ShapeDtypeStruct((n, D_MODEL), BF16),
                   jax.ShapeDtypeStruct((2 * PEER_HEADS, PEER_KEYS, n), F32)],
        compiler_params=_cparams(("parallel",)),
    )(x2, o_nsa, o_mla, gn, gm, *ws)


_CAND_ROWS = 16 + 7 * 8 + 8


def _take_top(v, k, idx, exact):
    n = v.shape[0]
    rank = jnp.full(v.shape, k, jnp.int32)
    vals = []
    for r in range(k):
        m = jnp.max(v, axis=0, keepdims=True)
        hit = v == m
        if exact:
            hit = idx == jnp.min(jnp.where(hit, idx, n), axis=0, keepdims=True)
        rank = jnp.where(hit, r, rank)
        v = jnp.where(hit, -jnp.inf, v)
        vals.append(m)
    return rank, vals


def _route(s1, s2, exact):
    k = PEER_TOPK
    tl = s1.shape[1]
    kidx = lax.broadcasted_iota(jnp.int32, (PEER_KEYS, tl), 0)
    rank1, a = _take_top(s1, k, kidx, exact)
    rank2, b = _take_top(s2, k, kidx, exact)
    bcat = jnp.concatenate(b, axis=0)
    groups = [a[0] + bcat]
    for p in range(1, 8):
        groups.append(a[p] + bcat[0:8])
    groups.append(jnp.concatenate(a[8:16], axis=0) + b[0])
    cand = jnp.concatenate(groups, axis=0)
    ridx = lax.broadcasted_iota(jnp.int32, (_CAND_ROWS, tl), 0)
    qpos = jnp.where(ridx < 16, ridx, (ridx - 16) & 7)
    ppos = jnp.where(ridx < 16, 0, ((ridx - 16) >> 3) + 1)
    valid = (ridx >= 16 + 7 * 8) | ((ppos + 1) * (qpos + 1) <= k)
    crank, _ = _take_top(jnp.where(valid, cand, -jnp.inf), k, ridx, exact)
    taken = crank < k
    takenf = taken.astype(F32)
    z = jnp.sum(jnp.where(taken, jnp.exp(cand - cand[0:1]), 0.0), axis=0, keepdims=True)
    cnt = jnp.zeros((PEER_KEYS, tl), F32)
    for p in range(k):
        if p == 0:
            c = jnp.sum(takenf[0:16], axis=0, keepdims=True)
        elif p < 8:
            c = jnp.sum(takenf[16 + (p - 1) * 8:16 + p * 8], axis=0, keepdims=True)
        else:
            c = takenf[72 + (p - 8):72 + (p - 8) + 1]
        cnt = jnp.where(rank1 == p, c, cnt)
    ea = jnp.where(rank1 < k, jnp.exp(s1 - a[0]), 0.0) / z
    eb = jnp.where(rank2 < k, jnp.exp(s2 - b[0]), 0.0)
    n_taken = (jnp.sum((rank1 < k).astype(F32), axis=0, keepdims=True)
               + jnp.sum((rank2 < k).astype(F32), axis=0, keepdims=True)
               + jnp.sum(takenf, axis=0, keepdims=True))
    return (cnt, ea, rank2.astype(F32).astype(BF16), eb.astype(BF16)), jnp.max(n_taken)


def _peer_topk_kernel(s_ref, cnt_out, ea_out, r2_out, eb_out):
    s1, s2 = s_ref[0], s_ref[1]
    outs = (cnt_out, ea_out, r2_out, eb_out)
    res, n_taken = _route(s1, s2, exact=False)
    for o, r in zip(outs, res):
        o[0] = r

    @pl.when(n_taken > 3 * PEER_TOPK)
    def _():
        res, _ = _route(s1, s2, exact=True)
        for o, r in zip(outs, res):
            o[0] = r


def _peer_topk(scores, *, tl=256):
    n = scores.shape[-1]
    ospec = pl.BlockSpec((1, PEER_KEYS, tl), lambda i, h: (h, 0, i))
    shp = (PEER_HEADS, PEER_KEYS, n)
    return pl.pallas_call(
        _peer_topk_kernel,
        grid=(n // tl, PEER_HEADS),
        in_specs=[pl.BlockSpec((2, PEER_KEYS, tl), lambda i, h: (h, 0, i))],
        out_specs=[ospec] * 4,
        out_shape=[jax.ShapeDtypeStruct(shp, F32), jax.ShapeDtypeStruct(shp, F32),
                   jax.ShapeDtypeStruct(shp, BF16), jax.ShapeDtypeStruct(shp, BF16)],
        compiler_params=_cparams(("parallel", "parallel")),
    )(scores)


BF16_ROWS = 16


def _peer_dense_kernel(xn_ref, d_ref, ut_ref, cnt_ref, ea_ref, r2_ref, eb_ref, y_out, g_ref, *, rows_per_step,
                       n_blocks):
    e = pl.program_id(1)
    tl = xn_ref.shape[0]
    chunk = 2 * PEER_KEYS

    def route_block(slot, up_slot=None):
        for c in range(rows_per_step // 2):
            if up_slot is not None:
                up_block(up_slot, c)
            hid = _dot_nt(d_ref[c * chunk:(c + 1) * chunk, :], xn_ref[...])
            for k2 in range(2):
                ii = 2 * c + k2
                w = None
                for hd in range(PEER_HEADS):
                    cnt = jnp.broadcast_to(cnt_ref[hd, ii:ii + 1, :], (BF16_ROWS, tl)).astype(BF16)
                    ea = jnp.broadcast_to(ea_ref[hd, ii:ii + 1, :], (BF16_ROWS, tl)).astype(BF16)
                    wh = jnp.where(r2_ref[hd] < cnt[None], eb_ref[hd] * ea[None], jnp.zeros((), BF16))
                    w = wh if w is None else w + wh
                act = _gelu_tanh(hid[k2 * PEER_KEYS:(k2 + 1) * PEER_KEYS, :]).astype(BF16)
                g_ref[slot, ii * PEER_KEYS:(ii + 1) * PEER_KEYS, :] = act * w.reshape(PEER_KEYS, tl)

    def up_block(slot, c):
        rows = slice(c * (D_MODEL // 4), (c + 1) * (D_MODEL // 4))
        y_out[rows, :] += _dot(ut_ref[rows, :], g_ref[slot])

    @pl.when(e == 0)
    def _():
        y_out[...] = jnp.zeros(y_out.shape, F32)
        route_block(0)

    @pl.when((e > 0) & (e < n_blocks))
    def _():
        route_block(e & 1, up_slot=1 - (e & 1))

    @pl.when(e == n_blocks)
    def _():
        for c in range(4):
            up_block(1 - (e & 1), c)


def _peer_dense(xn, d_bf, ut_bf, cnt, ea, r2, eb, *, tl=512, eb_rows=8):
    n = xn.shape[0]
    ebk = eb_rows * PEER_KEYS
    nb = PEER_EXPERTS // ebk
    split = lambda a: a.reshape(PEER_HEADS, PEER_KEYS // BF16_ROWS, BF16_ROWS, n)
    full = pl.BlockSpec((PEER_HEADS, PEER_KEYS // BF16_ROWS, BF16_ROWS, tl), lambda i, e: (0, 0, 0, i))
    part = pl.BlockSpec((PEER_HEADS, eb_rows, tl), lambda i, e: (0, jnp.minimum(e, nb - 1), i))
    return pl.pallas_call(
        functools.partial(_peer_dense_kernel, rows_per_step=eb_rows, n_blocks=nb),
        grid=(n // tl, nb + 1),
        in_specs=[pl.BlockSpec((tl, D_MODEL), lambda i, e: (i, 0)),
                  pl.BlockSpec((ebk, D_MODEL), lambda i, e: (jnp.minimum(e, nb - 1), 0)),
                  pl.BlockSpec((D_MODEL, ebk), lambda i, e: (0, jnp.maximum(e - 1, 0))),
                  part, part, full, full],
        out_specs=pl.BlockSpec((D_MODEL, tl), lambda i, e: (0, i)),
        out_shape=jax.ShapeDtypeStruct((D_MODEL, n), F32),
        scratch_shapes=[pltpu.VMEM((2, ebk, tl), BF16)],
        compiler_params=_cparams(("parallel", "arbitrary")),
    )(xn, d_bf, ut_bf, cnt, ea, split(r2), split(eb))


def _final_kernel(h_ref, yt_ref, g_ref, o_out):
    o_out[...] = _rms(h_ref[...] + yt_ref[...].T, g_ref[...])


def _final(h, yt, norm_f_g, *, tm=256):
    n = h.shape[0]
    return pl.pallas_call(
        _final_kernel,
        grid=(n // tm,),
        in_specs=[pl.BlockSpec((tm, D_MODEL), lambda i: (i, 0)), pl.BlockSpec((D_MODEL, tm), lambda i: (0, i)),
                  _const_spec((1, D_MODEL))],
        out_specs=pl.BlockSpec((tm, D_MODEL), lambda i: (i, 0)),
        out_shape=jax.ShapeDtypeStruct((n, D_MODEL), F32),
        compiler_params=_cparams(("parallel",)),
    )(h, yt, norm_f_g.reshape(1, -1))


def _layer(h2, pos2, batch, p):
    a = _inproj(h2, pos2, p['norm1_g'], p['w_in'], p['mla_q_norm_g'], p['mla_kv_norm_g'], p['mla_w_uq'],
                p['mla_w_ukv'])
    kc, vc = _compress(a['kc'], a['vc'], p['nsa_cmp_pos_k'], p['nsa_cmp_w1_k'], p['nsa_cmp_w2_k'],
                       p['nsa_cmp_pos_v'], p['nsa_cmp_w1_v'], p['nsa_cmp_w2_v'], batch)
    ocmp, selb = _nsa_cmp(a['q'], kc, vc, a['g3'], batch)
    o_nsa = _nsa_attn(a['q'], selb, a['ks'], a['vs'], a['kw'], a['vw'], a['g3'], ocmp, batch)
    o_mla = _mla_attn(a['qm'], a['km'], a['vm'], batch)
    h_mid, xn, scores = _mix(h2, o_nsa, o_mla, a['gn'], a['gm'], p['w_branch_nsa'], p['w_branch_mla'], p['w_out'],
                             p['norm2_g'], p['peer_w_query'], p['peer_sub_keys_1'], p['peer_sub_keys_2'])
    cnt, ea, r2, eb = _peer_topk(scores)
    yt = _peer_dense(xn, p['peer_expert_down'].astype(BF16), p['peer_expert_up'].T.astype(BF16), cnt, ea, r2, eb)
    return h_mid, yt


def kernel(x, positions, norm1_g, w_in, nsa_cmp_pos_k, nsa_cmp_w1_k, nsa_cmp_w2_k, nsa_cmp_pos_v, nsa_cmp_w1_v,
           nsa_cmp_w2_v, mla_q_norm_g, mla_kv_norm_g, mla_w_uq, mla_w_ukv, w_branch_nsa, w_branch_mla, w_out,
           norm2_g, peer_w_query, peer_sub_keys_1, peer_sub_keys_2, peer_expert_down, peer_expert_up, norm_f_g):
    batch, seq, d = x.shape
    assert seq == SEQ and d == D_MODEL
    stacked = dict(norm1_g=norm1_g, w_in=w_in, nsa_cmp_pos_k=nsa_cmp_pos_k, nsa_cmp_w1_k=nsa_cmp_w1_k,
                   nsa_cmp_w2_k=nsa_cmp_w2_k, nsa_cmp_pos_v=nsa_cmp_pos_v, nsa_cmp_w1_v=nsa_cmp_w1_v,
                   nsa_cmp_w2_v=nsa_cmp_w2_v, mla_q_norm_g=mla_q_norm_g, mla_kv_norm_g=mla_kv_norm_g,
                   mla_w_uq=mla_w_uq, mla_w_ukv=mla_w_ukv, w_branch_nsa=w_branch_nsa, w_branch_mla=w_branch_mla,
                   w_out=w_out, norm2_g=norm2_g, peer_w_query=peer_w_query, peer_sub_keys_1=peer_sub_keys_1,
                   peer_sub_keys_2=peer_sub_keys_2, peer_expert_down=peer_expert_down,
                   peer_expert_up=peer_expert_up)
    assert w_in.shape[0] == 1, "single-layer block"
    p = {k: v[0] for k, v in stacked.items()}
    h_mid, yt = _layer(x.reshape(batch * seq, d), positions.reshape(batch * seq, 1), batch, p)
    return _final(h_mid, yt, norm_f_g).reshape(batch, seq, d)
```

```python
import functools
import math

import numpy as np
import jax
import jax.numpy as jnp
from jax import lax
from jax.experimental import pallas as pl
from jax.experimental.pallas import tpu as pltpu

F32 = jnp.float32
BF16 = jnp.bfloat16

D_MODEL = 1024
SEQ = 2048
EPS = 1e-6
NEG = -1e30
FORCE_BONUS = 1e4

NSA_HEADS = 8
NSA_GROUPS = 2
NSA_HPG = 4
HEAD_DIM = 64
CMP_BLOCK = 32
CMP_STRIDE = 16
CMP_HIDDEN = 128
N_CMP = SEQ // CMP_STRIDE
SEL_BLOCK = 64
N_SEL = SEQ // SEL_BLOCK
SEL_TOPK = 16
N_LOCAL_FORCED = 2
WINDOW = 512

MLA_HEADS = 8
MLA_Q_LORA = 256
MLA_KV_LORA = 128
MLA_NOPE = 64
MLA_ROPE = 32
MLA_V = 64
ROPE_THETA = 10000.0

PEER_HEADS = 8
PEER_KEYS = 128
PEER_EXPERTS = PEER_KEYS * PEER_KEYS
PEER_QDIM = 256
PEER_TOPK = 16

LANE = 128
SEL_LANE0 = 64
POS_HI_LANE = 96
POS_LO_LANE = 97

VMEM_LIMIT = 56 * 1024 * 1024

IN_SPLITS = (512, 128, 128, 128, 128, 128, 128, 24, 256, 128, 32, 1024, 1024)


def _cparams(sem, flags=None):
    return pltpu.CompilerParams(dimension_semantics=sem, vmem_limit_bytes=VMEM_LIMIT, flags=flags)


def _dot(a, b):
    return jnp.dot(a, b, preferred_element_type=F32)


def _dot_nt(a, b):
    return lax.dot_general(a, b, (((1,), (1,)), ((), ())), preferred_element_type=F32)


def _rms(x, g):
    return x * lax.rsqrt(jnp.mean(x * x, axis=-1, keepdims=True) + EPS) * g


_GELU_K1 = -2.0 * math.sqrt(2.0 / math.pi) * math.log2(math.e)
_GELU_K2 = _GELU_K1 * 0.044715


def _gelu_tanh(x):
    return x / (1.0 + jnp.exp2(x * (_GELU_K1 + _GELU_K2 * (x * x))))


def _const_spec(shape):
    nd = len(shape)
    return pl.BlockSpec(shape, lambda *_: (0,) * nd)


_SEG = dict(q=1024, kc=128, vc=128, ks=256, kw=256, g3=128,
            cq=256, ckv=128, kr=128, krr=128, gn=1024, gm=1024)
_SEG_OFF = {}
_o = 0
for _k, _w in _SEG.items():
    _SEG_OFF[_k] = (_o, _o + _w)
    _o += _w
W_PAD = _o


def _pad_heads(w, n_heads, width):
    k = w.shape[0]
    return jnp.pad(w.reshape(k, n_heads, width), ((0, 0), (0, 0), (0, LANE - width))).reshape(k, n_heads * LANE)


def _dup_heads(w, n_heads, width):
    k = w.shape[0]
    w3 = w.reshape(k, n_heads, width)
    return jnp.concatenate([w3, w3], axis=-1).reshape(k, n_heads * 2 * width)


def _rope_rot_cols(w_pe):
    half = MLA_ROPE // 2
    return jnp.concatenate([-w_pe[..., half:], w_pe[..., :half]], axis=-1)


def _build_w_in(w_in):
    offs = np.cumsum((0,) + IN_SPLITS)
    seg = [w_in[:, offs[i]:offs[i + 1]] for i in range(len(IN_SPLITS))]
    wq, wkc, wvc, wks, wvs, wkw, wvw, wg, wcq, wckv, wkr, wgn, wgm = seg
    k = w_in.shape[0]
    g3 = jnp.pad(wg, ((0, 0), (0, LANE - 3 * NSA_HEADS)))
    kr = jnp.pad(wkr, ((0, 0), (SEL_LANE0, LANE - SEL_LANE0 - MLA_ROPE)))
    krr = jnp.pad(_rope_rot_cols(wkr), ((0, 0), (SEL_LANE0, LANE - SEL_LANE0 - MLA_ROPE)))
    cols = [_pad_heads(wq * (HEAD_DIM ** -0.5), NSA_HEADS, HEAD_DIM), wkc, wvc,
            _pad_heads(wks, NSA_GROUPS, HEAD_DIM), _pad_heads(wkw, NSA_GROUPS, HEAD_DIM),
            g3, wcq, wckv, kr, krr, wgn, wgm]
    w_vt = jnp.concatenate([_dup_heads(wvs, NSA_GROUPS, HEAD_DIM), _dup_heads(wvw, NSA_GROUPS, HEAD_DIM)], axis=1).T
    return jnp.concatenate(cols, axis=1).astype(BF16), w_vt.astype(BF16)


def _inproj_kernel(x_ref, pos_ref, g1_ref, w_ref, wvt_ref, qg_ref, kvg_ref, wuq_ref, wuqr_ref, wk_ref, wv_ref,
                   invl_ref, qconst_ref, gexp_ref,
                   q_out, kc_out, vc_out, ks_out, vs_out, kw_out, vw_out, g3_out,
                   qm_out, km_out, vm_out, gn_out, gm_out, *, tm):
    xn = _rms(x_ref[...], g1_ref[...]).astype(BF16)

    def proj(name, c0=0, c1=None):
        a, b = _SEG_OFF[name]
        c1 = b - a if c1 is None else c1
        return _dot(xn, w_ref[:, a + c0:a + c1])

    for h in range(NSA_HEADS):
        sl = slice(h * LANE, (h + 1) * LANE)
        q_out[:, sl] = (proj('q', h * LANE, (h + 1) * LANE) + qconst_ref[:, sl]).astype(BF16)
    kc_out[...] = proj('kc').astype(BF16)
    vc_out[...] = proj('vc').astype(BF16)

    row = lax.broadcasted_iota(jnp.int32, (tm, LANE), 0)
    lane = lax.broadcasted_iota(jnp.int32, (tm, LANE), 1)
    t = (pl.program_id(0) * tm + row) & (SEQ - 1)
    blk = t >> 6
    pos_aug = (jnp.where(lane == POS_HI_LANE, blk, 0) + jnp.where(lane == POS_LO_LANE, t & 63, 0)).astype(F32)
    sel_aug = pos_aug + jnp.where(lane == SEL_LANE0 + blk, 1.0, 0.0)
    for g in range(NSA_GROUPS):
        sl = slice(g * LANE, (g + 1) * LANE)
        ks_out[:, sl] = (proj('ks', g * LANE, (g + 1) * LANE) + sel_aug).astype(BF16)
        kw_out[:, sl] = (proj('kw', g * LANE, (g + 1) * LANE) + pos_aug).astype(BF16)
    vs_out[0] = _dot_nt(wvt_ref[0:2 * LANE, :], xn).astype(BF16)
    vw_out[0] = _dot_nt(wvt_ref[2 * LANE:4 * LANE, :], xn).astype(BF16)
    gate = jax.nn.sigmoid(proj('g3'))
    g_hi = gate.astype(BF16)
    g_lo = (gate - g_hi.astype(F32)).astype(BF16)
    for c in range(3):
        ex = gexp_ref[:, c * 512:(c + 1) * 512]
        g3_out[:, c * 512:(c + 1) * 512] = _dot(g_hi, ex) + _dot(g_lo, ex)
    for c in range(2):
        gn_out[:, c * 512:(c + 1) * 512] = jax.nn.sigmoid(proj('gn', c * 512, (c + 1) * 512))
        gm_out[:, c * 512:(c + 1) * 512] = jax.nn.sigmoid(proj('gm', c * 512, (c + 1) * 512))

    ang = pos_ref[...].astype(F32) * invl_ref[...]
    cos, sin = jnp.cos(ang), jnp.sin(ang)
    cqn = _rms(proj('cq'), qg_ref[...]).astype(BF16)
    ckvn = _rms(proj('ckv'), kvg_ref[...]).astype(BF16)
    kpe = proj('kr') * cos + proj('krr') * sin
    qscale = (MLA_NOPE + MLA_ROPE) ** -0.5
    for h in range(MLA_HEADS):
        sl = slice(h * LANE, (h + 1) * LANE)
        qh = _dot(cqn, wuq_ref[:, sl]) * cos + _dot(cqn, wuqr_ref[:, sl]) * sin
        qm_out[:, sl] = (qh * qscale).astype(BF16)
        km_out[:, sl] = (_dot(ckvn, wk_ref[:, sl]) + kpe).astype(BF16)
    vm_out[0] = _dot_nt(wv_ref[...], ckvn).astype(BF16)


def _inproj(x2, pos2, norm1_g, w_in, q_norm_g, kv_norm_g, w_uq, w_ukv, *, tm=256):
    n = x2.shape[0]
    w_pad, w_vt = _build_w_in(w_in)
    uq3 = w_uq.reshape(MLA_Q_LORA, MLA_HEADS, MLA_NOPE + MLA_ROPE)
    wuq = jnp.pad(uq3, ((0, 0), (0, 0), (0, LANE - MLA_NOPE - MLA_ROPE))).reshape(MLA_Q_LORA, -1).astype(BF16)
    uq_rot = jnp.concatenate([jnp.zeros_like(uq3[..., :MLA_NOPE]), _rope_rot_cols(uq3[..., MLA_NOPE:])], axis=-1)
    wuqr = jnp.pad(uq_rot, ((0, 0), (0, 0), (0, LANE - MLA_NOPE - MLA_ROPE))).reshape(MLA_Q_LORA, -1).astype(BF16)
    ukv3 = w_ukv.reshape(MLA_KV_LORA, MLA_HEADS, MLA_NOPE + MLA_V)
    wk = jnp.pad(ukv3[..., :MLA_NOPE], ((0, 0), (0, 0), (0, LANE - MLA_NOPE))).reshape(MLA_KV_LORA, -1).astype(BF16)
    wv = ukv3[..., MLA_NOPE:].reshape(MLA_KV_LORA, -1).T.astype(BF16)

    inv = ROPE_THETA ** (-np.arange(0, MLA_ROPE, 2, dtype=np.float32) / MLA_ROPE)
    invl = np.zeros((1, LANE), np.float32)
    invl[0, MLA_NOPE:MLA_NOPE + MLA_ROPE] = np.concatenate([inv, inv])
    slopes = 2.0 ** (-8.0 * np.arange(1, NSA_HEADS + 1, dtype=np.float32) / NSA_HEADS)
    qconst = np.zeros((1, NSA_HEADS, LANE), np.float32)
    qconst[0, :, POS_HI_LANE] = slopes * SEL_BLOCK
    qconst[0, :, POS_LO_LANE] = slopes
    qconst = qconst.reshape(1, NSA_HEADS * LANE)
    gexp = np.zeros((LANE, 3 * NSA_HEADS, HEAD_DIM), np.float32)
    gexp[np.arange(3 * NSA_HEADS), np.arange(3 * NSA_HEADS), :] = 1.0
    gexp = gexp.reshape(LANE, 3 * NSA_HEADS * HEAD_DIM)

    row = lambda w: pl.BlockSpec((tm, w), lambda i: (i, 0))
    outs = [('q', 1024, BF16), ('kc', 128, BF16), ('vc', 128, BF16), ('ks', 256, BF16), ('vs', -256, BF16),
            ('kw', 256, BF16), ('vw', -256, BF16), ('g3', 1536, F32), ('qm', 1024, BF16), ('km', 1024, BF16),
            ('vm', -512, BF16), ('gn', 1024, F32), ('gm', 1024, F32)]
    ospec = lambda w: row(w) if w > 0 else pl.BlockSpec((1, -w, tm), lambda i: (i, 0, 0))
    oshape = lambda w: (n, w) if w > 0 else (n // tm, -w, tm)
    res = pl.pallas_call(
        functools.partial(_inproj_kernel, tm=tm),
        grid=(n // tm,),
        in_specs=[row(D_MODEL), row(1), _const_spec((1, D_MODEL)), _const_spec((D_MODEL, W_PAD)),
                  _const_spec(w_vt.shape), _const_spec((1, MLA_Q_LORA)), _const_spec((1, MLA_KV_LORA)),
                  _const_spec(wuq.shape), _const_spec(wuqr.shape), _const_spec(wk.shape), _const_spec(wv.shape),
                  _const_spec((1, LANE)), _const_spec((1, NSA_HEADS * LANE)), _const_spec(gexp.shape)],
        out_specs=[ospec(w) for _, w, _ in outs],
        out_shape=[jax.ShapeDtypeStruct(oshape(w), dt) for _, w, dt in outs],
        compiler_params=_cparams(("parallel",)),
    )(x2, pos2, norm1_g.reshape(1, -1), w_pad, w_vt, q_norm_g.reshape(1, -1), kv_norm_g.reshape(1, -1),
      wuq, wuqr, wk, wv, jnp.asarray(invl), jnp.asarray(qconst), jnp.asarray(gexp, BF16))
    return dict(zip([o[0] for o in outs], res))


def _compress_kernel(zk_ref, zv_ref, pk_ref, pv_ref, w1k_ref, w1v_ref, w2k_ref, w2vt_ref, aug_ref,
                     kc_out, vc_out):
    half = CMP_STRIDE * NSA_GROUPS * HEAD_DIM

    def hidden(z_ref, p_ref, w1_ref):
        z = z_ref[0]
        top = _dot(z, w1_ref[0:half, :])
        bot = _dot(z, w1_ref[half:2 * half, :])
        pos = _dot(p_ref[:, 0:half], w1_ref[0:half, :]) + _dot(p_ref[:, half:2 * half], w1_ref[half:2 * half, :])
        pre = top + pltpu.roll(bot, N_CMP - 1, 0) + pos[0:1, :]
        return jax.nn.gelu(pre).astype(BF16)

    kc_out[0] = (_dot(hidden(zk_ref, pk_ref, w1k_ref), w2k_ref[...]) + aug_ref[...]).astype(BF16)
    vc_out[0] = _dot_nt(w2vt_ref[...], hidden(zv_ref, pv_ref, w1v_ref)).astype(BF16)


def _expand_w1(w1):
    w = w1.reshape(2, CMP_STRIDE, HEAD_DIM, CMP_HIDDEN)
    eye = jnp.eye(NSA_GROUPS, dtype=w1.dtype)
    w = jnp.einsum('hldc,ge->hlgdec', w, eye)
    return w.reshape(2 * CMP_STRIDE * NSA_GROUPS * HEAD_DIM, NSA_GROUPS * CMP_HIDDEN).astype(BF16)


def _expand_pos(pos):
    p = jnp.broadcast_to(pos.reshape(2, CMP_STRIDE, 1, HEAD_DIM), (2, CMP_STRIDE, NSA_GROUPS, HEAD_DIM))
    return jnp.broadcast_to(p.reshape(1, -1), (8, 2 * CMP_STRIDE * NSA_GROUPS * HEAD_DIM)).astype(BF16)


def _expand_w2(w2, dup):
    blk = jnp.concatenate([w2, w2 if dup else jnp.zeros_like(w2)], axis=-1)
    eye = jnp.eye(NSA_GROUPS, dtype=w2.dtype)
    return jnp.einsum('cd,ge->gced', blk, eye).reshape(NSA_GROUPS * CMP_HIDDEN, NSA_GROUPS * LANE).astype(BF16)


def _compress(kc_raw, vc_raw, pos_k, w1_k, w2_k, pos_v, w1_v, w2_v, batch):
    zk = kc_raw.reshape(batch, N_CMP, CMP_STRIDE * LANE)
    zv = vc_raw.reshape(batch, N_CMP, CMP_STRIDE * LANE)
    endpos = np.arange(N_CMP) * CMP_STRIDE + CMP_BLOCK - 1
    aug = np.zeros((N_CMP, NSA_GROUPS, LANE), np.float32)
    aug[:, :, POS_HI_LANE] = (endpos // SEL_BLOCK)[:, None]
    aug[:, :, POS_LO_LANE] = (endpos % SEL_BLOCK)[:, None]
    aug = aug.reshape(N_CMP, NSA_GROUPS * LANE)
    zspec = pl.BlockSpec((1, N_CMP, CMP_STRIDE * LANE), lambda b: (b, 0, 0))
    ospec = pl.BlockSpec((1, N_CMP, NSA_GROUPS * LANE), lambda b: (b, 0, 0))
    args = (zk, zv, _expand_pos(pos_k), _expand_pos(pos_v), _expand_w1(w1_k), _expand_w1(w1_v),
            _expand_w2(w2_k, False), _expand_w2(w2_v, True).T, jnp.asarray(aug))
    return pl.pallas_call(
        _compress_kernel,
        grid=(batch,),
        in_specs=[zspec, zspec] + [_const_spec(a.shape) for a in args[2:]],
        out_specs=[ospec, pl.BlockSpec((1, NSA_GROUPS * LANE, N_CMP), lambda b: (b, 0, 0))],
        out_shape=[jax.ShapeDtypeStruct((batch, N_CMP, NSA_GROUPS * LANE), BF16),
                   jax.ShapeDtypeStruct((batch, NSA_GROUPS * LANE, N_CMP), BF16)],
        compiler_params=_cparams(("parallel",)),
    )(*args)


def _nsa_cmp_kernel(q_ref, kc_ref, vct_ref, gate_ref, ovt_ref, ocmp_out, selb_out, *, tq):
    qi = pl.program_id(1)
    slot = lax.broadcasted_iota(jnp.int32, (N_CMP, tq), 0)
    tslot = qi * tq + lax.broadcasted_iota(jnp.int32, (N_CMP, tq), 1)
    visible = tslot >= slot * CMP_STRIDE + (CMP_BLOCK - 1)
    sblk = lax.broadcasted_iota(jnp.int32, (N_SEL, tq), 0)
    back = ((qi * tq + lax.broadcasted_iota(jnp.int32, (N_SEL, tq), 1)) >> 6) - sblk
    causal = back >= 0
    forced = causal & ((back < N_LOCAL_FORCED) | (sblk == 0))
    low_rows = lax.broadcasted_iota(jnp.int32, (LANE, tq), 0) < HEAD_DIM

    for g in range(NSA_GROUPS):
        kc = kc_ref[0, :, g * LANE:(g + 1) * LANE]
        vct = vct_ref[0, g * LANE:(g + 1) * LANE, :]
        imp = jnp.zeros((N_SEL, tq), F32)
        outs = []
        for hh in range(NSA_HPG):
            h = g * NSA_HPG + hh
            s = jnp.where(visible, _dot_nt(kc, q_ref[:, h * LANE:(h + 1) * LANE]), NEG)
            m = jnp.max(s, axis=0, keepdims=True)
            p = jnp.where(visible, jnp.exp(s - m), 0.0)
            p = p / jnp.maximum(jnp.sum(p, axis=0, keepdims=True), 1e-30)
            pb = p.astype(BF16)
            outs.append(_dot(vct, pb))
            imp = imp + _dot(ovt_ref[...], pb)
        for pr in range(NSA_HPG // 2):
            c0 = (g * 2 + pr) * LANE
            o = jnp.where(low_rows, outs[2 * pr], outs[2 * pr + 1]).T
            ocmp_out[:, c0:c0 + LANE] = o * gate_ref[:, c0:c0 + LANE]

        score = jnp.where(causal, imp + jnp.where(forced, FORCE_BONUS, 0.0), NEG)
        rank = jnp.zeros((N_SEL, tq), jnp.int32)
        for s2 in range(N_SEL):
            other = score[s2:s2 + 1]
            ahead = (other > score) | ((other == score) & (sblk > s2))
            rank = rank + ahead.astype(jnp.int32)
        bias = jnp.where(causal & (rank < SEL_TOPK), 0.0, NEG)
        placed = jnp.concatenate([jnp.zeros((SEL_LANE0, tq), F32), bias,
                                  jnp.zeros((LANE - SEL_LANE0 - N_SEL, tq), F32)], axis=0)
        selb_out[:, g * LANE:(g + 1) * LANE] = placed.T.astype(BF16)


def _nsa_cmp(q, kc, vct, g3, batch, *, tq=256):
    n = q.shape[0]
    nq = SEQ // tq
    cmp_start = np.arange(N_CMP) * CMP_STRIDE
    sel_start = np.arange(N_SEL) * SEL_BLOCK
    ov = np.clip(np.minimum(cmp_start[:, None] + CMP_BLOCK, sel_start[None, :] + SEL_BLOCK)
                 - np.maximum(cmp_start[:, None], sel_start[None, :]), 0, None).astype(np.float32) / CMP_BLOCK
    rows = lambda w: pl.BlockSpec((tq, w), lambda b, i: (b * nq + i, 0))
    return pl.pallas_call(
        functools.partial(_nsa_cmp_kernel, tq=tq),
        grid=(batch, nq),
        in_specs=[rows(1024), pl.BlockSpec((1, N_CMP, NSA_GROUPS * LANE), lambda b, i: (b, 0, 0)),
                  pl.BlockSpec((1, NSA_GROUPS * LANE, N_CMP), lambda b, i: (b, 0, 0)), rows(512),
                  _const_spec((N_SEL, N_CMP))],
        out_specs=[rows(512), rows(NSA_GROUPS * LANE)],
        out_shape=[jax.ShapeDtypeStruct((n, 512), F32), jax.ShapeDtypeStruct((n, NSA_GROUPS * LANE), BF16)],
        compiler_params=_cparams(("parallel", "parallel")),
    )(q, kc, vct, g3, jnp.asarray(ov.T, BF16))


def _flash_init(m_ref, l_ref, acc_ref):
    m_ref[...] = jnp.full(m_ref.shape, NEG, F32)
    l_ref[...] = jnp.zeros(l_ref.shape, F32)
    acc_ref[...] = jnp.zeros(acc_ref.shape, F32)


def _flash_update(st, vt, m_ref, l_ref, acc_ref):
    m_old = m_ref[...]
    m_new = jnp.maximum(m_old, jnp.max(st, axis=0, keepdims=True))
    alpha = jnp.exp(m_old - m_new)
    p = jnp.exp(st - m_new)
    l_ref[...] = alpha * l_ref[...] + jnp.sum(p, axis=0, keepdims=True)
    acc_ref[...] = alpha * acc_ref[...] + _dot(vt, p.astype(BF16))
    m_ref[...] = m_new


def _tile_iotas(heads, t):
    key = lax.broadcasted_iota(jnp.int32, (t, heads * t), 0)
    qry = lax.broadcasted_iota(jnp.int32, (t, heads * t), 1) & (t - 1)
    return key, qry


def _nsa_attn_kernel(q_ref, selb_ref, ks_ref, vs_ref, kw_ref, vw_ref, gs_ref, gw_ref, ocmp_ref, o_out,
                     m_ref, l_ref, acc_ref, *, t):
    qi = pl.program_id(2)
    key, qry = _tile_iotas(NSA_HPG, t)
    qs = [q_ref[:, hh * LANE:(hh + 1) * LANE] for hh in range(NSA_HPG)]
    q_win = jnp.concatenate(qs, axis=0)
    q_sel = jnp.concatenate([q + selb_ref[...] for q in qs], axis=0)
    low_rows = lax.broadcasted_iota(jnp.int32, (LANE, t), 0) < HEAD_DIM

    def keys(ref, j):
        return ref[pl.ds(pl.multiple_of(j * t, t), t), :]

    def finish():
        o = acc_ref[...] / l_ref[...]
        return [jnp.where(low_rows, o[:, (2 * p) * t:(2 * p + 1) * t], o[:, (2 * p + 1) * t:(2 * p + 2) * t]).T
                for p in range(NSA_HPG // 2)]

    _flash_init(m_ref, l_ref, acc_ref)
    s = _dot_nt(keys(ks_ref, qi), q_sel)
    _flash_update(jnp.where(key <= qry, s, NEG), vs_ref[qi], m_ref, l_ref, acc_ref)

    def sel_body(j, carry):
        _flash_update(_dot_nt(keys(ks_ref, j), q_sel), vs_ref[j], m_ref, l_ref, acc_ref)
        return carry

    lax.fori_loop(0, qi, sel_body, 0)
    o_sel = finish()

    _flash_init(m_ref, l_ref, acc_ref)
    s = _dot_nt(keys(kw_ref, qi), q_win)
    _flash_update(jnp.where(key <= qry, s, NEG), vw_ref[qi], m_ref, l_ref, acc_ref)

    @pl.when(qi >= 1)
    def _():
        _flash_update(_dot_nt(keys(kw_ref, qi - 1), q_win), vw_ref[qi - 1], m_ref, l_ref, acc_ref)

    @pl.when(qi >= 2)
    def _():
        s = _dot_nt(keys(kw_ref, qi - 2), q_win)
        _flash_update(jnp.where(key > qry, s, NEG), vw_ref[qi - 2], m_ref, l_ref, acc_ref)

    o_win = finish()
    for p in range(NSA_HPG // 2):
        sl = slice(p * LANE, (p + 1) * LANE)
        o_out[:, sl] = (gs_ref[:, sl] * o_sel[p] + gw_ref[:, sl] * o_win[p] + ocmp_ref[:, sl]).astype(BF16)


def _nsa_attn(q, selb, ks, vs, kw, vw, g3, ocmp, batch, *, t=256):
    assert WINDOW == 2 * t
    n = q.shape[0]
    nq = SEQ // t
    gw = NSA_HPG * HEAD_DIM
    rows = lambda w, off=0: pl.BlockSpec((t, w), lambda b, g, i: (b * nq + i, g + off))
    seq = pl.BlockSpec((SEQ, LANE), lambda b, g, i: (b, g))
    seq_t = pl.BlockSpec((nq, LANE, t), lambda b, g, i: (b, g, 0))
    m = NSA_HPG * t
    return pl.pallas_call(
        functools.partial(_nsa_attn_kernel, t=t),
        grid=(batch, NSA_GROUPS, nq),
        in_specs=[rows(NSA_HPG * LANE), rows(LANE), seq, seq_t, seq, seq_t,
                  rows(gw, 2), rows(gw, 4), rows(gw)],
        out_specs=rows(gw),
        out_shape=jax.ShapeDtypeStruct((n, NSA_HEADS * HEAD_DIM), BF16),
        scratch_shapes=[pltpu.VMEM((1, m), F32), pltpu.VMEM((1, m), F32), pltpu.VMEM((LANE, m), F32)],
        compiler_params=_cparams(("parallel", "parallel", "arbitrary")),
    )(q, selb, ks, vs, kw, vw, g3, g3, ocmp)


def _mla_attn_kernel(q_ref, k_ref, v_ref, o_out, m_ref, l_ref, acc_ref, *, tq, tk):
    qi = pl.program_id(2)
    per = tq // tk
    key = lax.broadcasted_iota(jnp.int32, (tk, 2 * tq), 0)
    qry = lax.broadcasted_iota(jnp.int32, (tk, 2 * tq), 1) & (tq - 1)
    q0, q1 = q_ref[:, 0:LANE], q_ref[:, LANE:2 * LANE]

    def scores(j):
        k = k_ref[pl.ds(pl.multiple_of(j * tk, tk), tk), :]
        return jnp.concatenate([_dot_nt(k[:, 0:LANE], q0), _dot_nt(k[:, LANE:2 * LANE], q1)], axis=1)

    _flash_init(m_ref, l_ref, acc_ref)
    for d in range(per):
        j = qi * per + d
        _flash_update(jnp.where(key + d * tk <= qry, scores(j), NEG), v_ref[j], m_ref, l_ref, acc_ref)

    def body(j, carry):
        _flash_update(scores(j), v_ref[j], m_ref, l_ref, acc_ref)
        return carry

    lax.fori_loop(0, qi * per, body, 0)
    o = acc_ref[...] / l_ref[...]
    low_rows = lax.broadcasted_iota(jnp.int32, (LANE, tq), 0) < MLA_V
    o_out[...] = jnp.where(low_rows, o[:, 0:tq], o[:, tq:2 * tq]).T.astype(BF16)


def _mla_attn(qm, km, vm, batch, *, tq=512):
    n = qm.shape[0]
    nk, _, tk = vm.shape
    nq = SEQ // tq
    return pl.pallas_call(
        functools.partial(_mla_attn_kernel, tq=tq, tk=tk),
        grid=(batch, MLA_HEADS // 2, nq),
        in_specs=[pl.BlockSpec((tq, 2 * LANE), lambda b, p, i: (b * nq + i, p)),
                  pl.BlockSpec((SEQ, 2 * LANE), lambda b, p, i: (b, p)),
                  pl.BlockSpec((SEQ // tk, LANE, tk), lambda b, p, i: (b, p, 0))],
        out_specs=pl.BlockSpec((tq, LANE), lambda b, p, i: (b * nq + i, p)),
        out_shape=jax.ShapeDtypeStruct((n, MLA_HEADS * MLA_V), BF16),
        scratch_shapes=[pltpu.VMEM((1, 2 * tq), F32), pltpu.VMEM((1, 2 * tq), F32),
                        pltpu.VMEM((LANE, 2 * tq), F32)],
        compiler_params=_cparams(("parallel", "parallel", "arbitrary")),
    )(qm, km, vm)


def _mix_kernel(x_ref, on_ref, om_ref, gn_ref, gm_ref, wn_ref, wm_ref, wo_ref, g2_ref, wq_ref, k1_ref, k2_ref,
                h_out, xn_out, s_out):
    merged = gn_ref[...] * _dot(on_ref[...], wn_ref[...]) + gm_ref[...] * _dot(om_ref[...], wm_ref[...])
    h = x_ref[...] + _dot(merged.astype(BF16), wo_ref[...])
    h_out[...] = h
    xn32 = _rms(h, g2_ref[...])
    xn = xn32.astype(BF16)
    xn_out[...] = xn32.T.astype(BF16)
    half = PEER_QDIM // 2
    for hd in range(PEER_HEADS):
        q = _dot(xn, wq_ref[:, hd * PEER_QDIM:(hd + 1) * PEER_QDIM]).astype(BF16)
        s_out[2 * hd] = _dot_nt(k1_ref[...], q[:, 0:half])
        s_out[2 * hd + 1] = _dot_nt(k2_ref[...], q[:, half:PEER_QDIM])


def _mix(x2, o_nsa, o_mla, gn, gm, w_n, w_m, w_o, norm2_g, w_q, keys1, keys2, *, tm=256):
    n = x2.shape[0]
    row = lambda w: pl.BlockSpec((tm, w), lambda i: (i, 0))
    ws = [w_n.astype(BF16), w_m.astype(BF16), w_o.astype(BF16), norm2_g.reshape(1, -1), w_q.astype(BF16),
          keys1.astype(BF16), keys2.astype(BF16)]
    return pl.pallas_call(
        _mix_kernel,
        grid=(n // tm,),
        in_specs=[row(D_MODEL), row(512), row(512), row(D_MODEL), row(D_MODEL)] + [_const_spec(w.shape) for w in ws],
        out_specs=[row(D_MODEL), pl.BlockSpec((D_MODEL, tm), lambda i: (0, i)),
                   pl.BlockSpec((2 * PEER_HEADS, PEER_KEYS, tm), lambda i: (0, 0, i))],
        out_shape=[jax.ShapeDtypeStruct((n, D_MODEL), F32), jax.ShapeDtypeStruct((D_MODEL, n), BF16),
                   jax.ShapeDtypeStruct((2 * PEER_HEADS, PEER_KEYS, n), F32)],
        compiler_params=_cparams(("parallel",)),
    )(x2, o_nsa, o_mla, gn, gm, *ws)


_CAND_ROWS = 16 + 7 * 8 + 8


def _take_top(v, k, idx, exact):
    n = v.shape[0]
    rank = jnp.full(v.shape, k, jnp.int32)
    vals = []
    for r in range(k):
        m = jnp.max(v, axis=0, keepdims=True)
        hit = v == m
        if exact:
            hit = idx == jnp.min(jnp.where(hit, idx, n), axis=0, keepdims=True)
        rank = jnp.where(hit, r, rank)
        v = jnp.where(hit, -jnp.inf, v)
        vals.append(m)
    return rank, vals


def _route(s1, s2, exact):
    k = PEER_TOPK
    tl = s1.shape[1]
    kidx = lax.broadcasted_iota(jnp.int32, (PEER_KEYS, tl), 0)
    rank1, a = _take_top(s1, k, kidx, exact)
    rank2, b = _take_top(s2, k, kidx, exact)
    bcat = jnp.concatenate(b, axis=0)
    groups = [a[0] + bcat]
    for p in range(1, 8):
        groups.append(a[p] + bcat[0:8])
    groups.append(jnp.concatenate(a[8:16], axis=0) + b[0])
    cand = jnp.concatenate(groups, axis=0)
    ridx = lax.broadcasted_iota(jnp.int32, (_CAND_ROWS, tl), 0)
    qpos = jnp.where(ridx < 16, ridx, (ridx - 16) & 7)
    ppos = jnp.where(ridx < 16, 0, ((ridx - 16) >> 3) + 1)
    valid = (ridx >= 16 + 7 * 8) | ((ppos + 1) * (qpos + 1) <= k)
    crank, _ = _take_top(jnp.where(valid, cand, -jnp.inf), k, ridx, exact)
    taken = crank < k
    takenf = taken.astype(F32)
    z = jnp.sum(jnp.where(taken, jnp.exp(cand - cand[0:1]), 0.0), axis=0, keepdims=True)
    cnt = jnp.zeros((PEER_KEYS, tl), F32)
    for p in range(k):
        if p == 0:
            c = jnp.sum(takenf[0:16], axis=0, keepdims=True)
        elif p < 8:
            c = jnp.sum(takenf[16 + (p - 1) * 8:16 + p * 8], axis=0, keepdims=True)
        else:
            c = takenf[72 + (p - 8):72 + (p - 8) + 1]
        cnt = jnp.where(rank1 == p, c, cnt)
    ea = jnp.where(rank1 < k, jnp.exp(s1 - a[0]), 0.0) / z
    eb = jnp.where(rank2 < k, jnp.exp(s2 - b[0]), 0.0)
    n_taken = (jnp.sum((rank1 < k).astype(F32), axis=0, keepdims=True)
               + jnp.sum((rank2 < k).astype(F32), axis=0, keepdims=True)
               + jnp.sum(takenf, axis=0, keepdims=True))
    return (cnt, ea, rank2.astype(F32).astype(BF16), eb.astype(BF16)), jnp.max(n_taken)


def _peer_topk_kernel(s_ref, cnt_out, ea_out, r2_out, eb_out):
    s1, s2 = s_ref[0], s_ref[1]
    outs = (cnt_out, ea_out, r2_out, eb_out)
    res, n_taken = _route(s1, s2, exact=False)
    for o, r in zip(outs, res):
        o[0] = r

    @pl.when(n_taken > 3 * PEER_TOPK)
    def _():
        res, _ = _route(s1, s2, exact=True)
        for o, r in zip(outs, res):
            o[0] = r


def _peer_topk(scores, *, tl=256):
    n = scores.shape[-1]
    ospec = pl.BlockSpec((1, PEER_KEYS, tl), lambda i, h: (h, 0, i))
    shp = (PEER_HEADS, PEER_KEYS, n)
    return pl.pallas_call(
        _peer_topk_kernel,
        grid=(n // tl, PEER_HEADS),
        in_specs=[pl.BlockSpec((2, PEER_KEYS, tl), lambda i, h: (h, 0, i))],
        out_specs=[ospec] * 4,
        out_shape=[jax.ShapeDtypeStruct(shp, F32), jax.ShapeDtypeStruct(shp, F32),
                   jax.ShapeDtypeStruct(shp, BF16), jax.ShapeDtypeStruct(shp, BF16)],
        compiler_params=_cparams(("parallel", "parallel")),
    )(scores)


BF16_ROWS = 16


def _peer_dense_kernel(xnt_ref, d_ref, ut_ref, cnt_ref, ea_ref, r2_ref, eb_ref, y_out, g_ref, *, rows_per_step):
    e = pl.program_id(1)
    tl = xnt_ref.shape[1]
    chunk = 2 * PEER_KEYS

    @pl.when(e == 0)
    def _():
        y_out[...] = jnp.zeros(y_out.shape, F32)

    for c in range(rows_per_step // 2):
        hid = _dot(d_ref[c * chunk:(c + 1) * chunk, :], xnt_ref[...])
        for k2 in range(2):
            ii = 2 * c + k2
            w = None
            for hd in range(PEER_HEADS):
                cnt = jnp.broadcast_to(cnt_ref[hd, ii:ii + 1, :], (BF16_ROWS, tl)).astype(BF16)
                ea = jnp.broadcast_to(ea_ref[hd, ii:ii + 1, :], (BF16_ROWS, tl)).astype(BF16)
                wh = jnp.where(r2_ref[hd] < cnt[None], eb_ref[hd] * ea[None], jnp.zeros((), BF16))
                w = wh if w is None else w + wh
            act = _gelu_tanh(hid[k2 * PEER_KEYS:(k2 + 1) * PEER_KEYS, :]).astype(BF16)
            g_ref[ii * PEER_KEYS:(ii + 1) * PEER_KEYS, :] = act * w.reshape(PEER_KEYS, tl)
    y_out[...] += _dot(ut_ref[...], g_ref[...])


def _peer_dense(xnt, d_bf, ut_bf, cnt, ea, r2, eb, *, tl=512, eb_rows=8):
    n = xnt.shape[1]
    ebk = eb_rows * PEER_KEYS
    groups = PEER_KEYS // BF16_ROWS
    split = lambda a: a.reshape(PEER_HEADS, groups, BF16_ROWS, n)
    full = pl.BlockSpec((PEER_HEADS, groups, BF16_ROWS, tl), lambda i, e: (0, 0, 0, i))
    part = pl.BlockSpec((PEER_HEADS, eb_rows, tl), lambda i, e: (0, e, i))
    return pl.pallas_call(
        functools.partial(_peer_dense_kernel, rows_per_step=eb_rows),
        grid=(n // tl, PEER_EXPERTS // ebk),
        in_specs=[pl.BlockSpec((D_MODEL, tl), lambda i, e: (0, i)),
                  pl.BlockSpec((ebk, D_MODEL), lambda i, e: (e, 0)),
                  pl.BlockSpec((D_MODEL, ebk), lambda i, e: (0, e)),
                  part, part, full, full],
        out_specs=pl.BlockSpec((D_MODEL, tl), lambda i, e: (0, i)),
        out_shape=jax.ShapeDtypeStruct((D_MODEL, n), F32),
        scratch_shapes=[pltpu.VMEM((ebk, tl), BF16)],
        compiler_params=_cparams(("parallel", "arbitrary")),
    )(xnt, d_bf, ut_bf, cnt, ea, split(r2), split(eb))


def _final_kernel(h_ref, yt_ref, g_ref, o_out):
    o_out[...] = _rms(h_ref[...] + yt_ref[...].T, g_ref[...])


def _final(h, yt, norm_f_g, *, tm=256):
    n = h.shape[0]
    return pl.pallas_call(
        _final_kernel,
        grid=(n // tm,),
        in_specs=[pl.BlockSpec((tm, D_MODEL), lambda i: (i, 0)), pl.BlockSpec((D_MODEL, tm), lambda i: (0, i)),
                  _const_spec((1, D_MODEL))],
        out_specs=pl.BlockSpec((tm, D_MODEL), lambda i: (i, 0)),
        out_shape=jax.ShapeDtypeStruct((n, D_MODEL), F32),
        compiler_params=_cparams(("parallel",)),
    )(h, yt, norm_f_g.reshape(1, -1))


def _layer(h2, pos2, batch, p):
    a = _inproj(h2, pos2, p['norm1_g'], p['w_in'], p['mla_q_norm_g'], p['mla_kv_norm_g'], p['mla_w_uq'],
                p['mla_w_ukv'])
    kc, vc = _compress(a['kc'], a['vc'], p['nsa_cmp_pos_k'], p['nsa_cmp_w1_k'], p['nsa_cmp_w2_k'],
                       p['nsa_cmp_pos_v'], p['nsa_cmp_w1_v'], p['nsa_cmp_w2_v'], batch)
    ocmp, selb = _nsa_cmp(a['q'], kc, vc, a['g3'], batch)
    o_nsa = _nsa_attn(a['q'], selb, a['ks'], a['vs'], a['kw'], a['vw'], a['g3'], ocmp, batch)
    o_mla = _mla_attn(a['qm'], a['km'], a['vm'], batch)
    h_mid, xn, scores = _mix(h2, o_nsa, o_mla, a['gn'], a['gm'], p['w_branch_nsa'], p['w_branch_mla'], p['w_out'],
                             p['norm2_g'], p['peer_w_query'], p['peer_sub_keys_1'], p['peer_sub_keys_2'])
    cnt, ea, r2, eb = _peer_topk(scores)
    yt = _peer_dense(xn, p['peer_expert_down'].astype(BF16), p['peer_expert_up'].T.astype(BF16), cnt, ea, r2, eb)
    return h_mid, yt


def kernel(x, positions, norm1_g, w_in, nsa_cmp_pos_k, nsa_cmp_w1_k, nsa_cmp_w2_k, nsa_cmp_pos_v, nsa_cmp_w1_v,
           nsa_cmp_w2_v, mla_q_norm_g, mla_kv_norm_g, mla_w_uq, mla_w_ukv, w_branch_nsa, w_branch_mla, w_out,
           norm2_g, peer_w_query, peer_sub_keys_1, peer_sub_keys_2, peer_expert_down, peer_expert_up, norm_f_g):
    batch, seq, d = x.shape
    assert seq == SEQ and d == D_MODEL
    stacked = dict(norm1_g=norm1_g, w_in=w_in, nsa_cmp_pos_k=nsa_cmp_pos_k, nsa_cmp_w1_k=nsa_cmp_w1_k,
                   nsa_cmp_w2_k=nsa_cmp_w2_k, nsa_cmp_pos_v=nsa_cmp_pos_v, nsa_cmp_w1_v=nsa_cmp_w1_v,
                   nsa_cmp_w2_v=nsa_cmp_w2_v, mla_q_norm_g=mla_q_norm_g, mla_kv_norm_g=mla_kv_norm_g,
                   mla_w_uq=mla_w_uq, mla_w_ukv=mla_w_ukv, w_branch_nsa=w_branch_nsa, w_branch_mla=w_branch_mla,
                   w_out=w_out, norm2_g=norm2_g, peer_w_query=peer_w_query, peer_sub_keys_1=peer_sub_keys_1,
                   peer_sub_keys_2=peer_sub_keys_2, peer_expert_down=peer_expert_down,
                   peer_expert_up=peer_expert_up)
    assert w_in.shape[0] == 1, "single-layer block"
    p = {k: v[0] for k, v in stacked.items()}
    h_mid, yt = _layer(x.reshape(batch * seq, d), positions.reshape(batch * seq, 1), batch, p)
    return _final(h_mid, yt, norm_f_g).reshape(batch, seq, d)
```

```python
import functools
import math

import numpy as np
import jax
import jax.numpy as jnp
from jax import lax
from jax.experimental import pallas as pl
from jax.experimental.pallas import tpu as pltpu

F32 = jnp.float32
BF16 = jnp.bfloat16

D_MODEL = 1024
SEQ = 2048
EPS = 1e-6
NEG = -1e30
FORCE_BONUS = 1e4

NSA_HEADS = 8
NSA_GROUPS = 2
NSA_HPG = 4
HEAD_DIM = 64
CMP_BLOCK = 32
CMP_STRIDE = 16
CMP_HIDDEN = 128
N_CMP = SEQ // CMP_STRIDE
SEL_BLOCK = 64
N_SEL = SEQ // SEL_BLOCK
SEL_TOPK = 16
N_LOCAL_FORCED = 2
WINDOW = 512

MLA_HEADS = 8
MLA_Q_LORA = 256
MLA_KV_LORA = 128
MLA_NOPE = 64
MLA_ROPE = 32
MLA_V = 64
ROPE_THETA = 10000.0

PEER_HEADS = 8
PEER_KEYS = 128
PEER_EXPERTS = PEER_KEYS * PEER_KEYS
PEER_QDIM = 256
PEER_TOPK = 16

LANE = 128
BF16_ROWS = 16
VT_ROWS = LANE + BF16_ROWS
SEL_LANE0 = 64
POS_LANE0 = 96
POS_PIECES = 3
LOG2E = math.log2(math.e)

VMEM_LIMIT = 56 * 1024 * 1024

IN_SPLITS = (512, 128, 128, 128, 128, 128, 128, 24, 256, 128, 32, 1024, 1024)


def _cparams(sem, flags=None):
    return pltpu.CompilerParams(dimension_semantics=sem, vmem_limit_bytes=VMEM_LIMIT, flags=flags)


def _dot(a, b):
    return jnp.dot(a, b, preferred_element_type=F32)


def _dot_nt(a, b):
    return lax.dot_general(a, b, (((1,), (1,)), ((), ())), preferred_element_type=F32)


def _rms(x, g):
    return x * lax.rsqrt(jnp.mean(x * x, axis=-1, keepdims=True) + EPS) * g


_GELU_K1 = -2.0 * math.sqrt(2.0 / math.pi) * math.log2(math.e)
_GELU_K2 = _GELU_K1 * 0.044715


def _gelu_tanh(x):
    return x / (1.0 + jnp.exp2(x * (_GELU_K1 + _GELU_K2 * (x * x))))


def _const_spec(shape):
    nd = len(shape)
    return pl.BlockSpec(shape, lambda *_: (0,) * nd)


_SEG = dict(q=1024, kc=128, vc=128, ks=256, kw=256, g3=128,
            cq=256, ckv=128, kr=128, krr=128, gn=1024, gm=1024)
_SEG_OFF = {}
_o = 0
for _k, _w in _SEG.items():
    _SEG_OFF[_k] = (_o, _o + _w)
    _o += _w
W_PAD = _o


def _pad_heads(w, n_heads, width):
    k = w.shape[0]
    return jnp.pad(w.reshape(k, n_heads, width), ((0, 0), (0, 0), (0, LANE - width))).reshape(k, n_heads * LANE)


def _dup_heads(w, n_heads, width):
    k = w.shape[0]
    w3 = w.reshape(k, n_heads, width)
    return jnp.concatenate([w3, w3], axis=-1).reshape(k, n_heads * 2 * width)


def _rope_rot_cols(w_pe):
    half = MLA_ROPE // 2
    return jnp.concatenate([-w_pe[..., half:], w_pe[..., :half]], axis=-1)


def _build_w_in(w_in):
    offs = np.cumsum((0,) + IN_SPLITS)
    seg = [w_in[:, offs[i]:offs[i + 1]] for i in range(len(IN_SPLITS))]
    wq, wkc, wvc, wks, wvs, wkw, wvw, wg, wcq, wckv, wkr, wgn, wgm = seg
    k = w_in.shape[0]
    g3 = jnp.pad(wg, ((0, 0), (0, LANE - 3 * NSA_HEADS)))
    kr = jnp.pad(wkr, ((0, 0), (SEL_LANE0, LANE - SEL_LANE0 - MLA_ROPE)))
    krr = jnp.pad(_rope_rot_cols(wkr), ((0, 0), (SEL_LANE0, LANE - SEL_LANE0 - MLA_ROPE)))
    cols = [_pad_heads(wq * (HEAD_DIM ** -0.5 * LOG2E), NSA_HEADS, HEAD_DIM), wkc, wvc,
            _pad_heads(wks, NSA_GROUPS, HEAD_DIM), _pad_heads(wkw, NSA_GROUPS, HEAD_DIM),
            g3, wcq, wckv, kr, krr, wgn, wgm]
    w_vt = jnp.concatenate([_dup_heads(wvs, NSA_GROUPS, HEAD_DIM), _dup_heads(wvw, NSA_GROUPS, HEAD_DIM)], axis=1).T
    return jnp.concatenate(cols, axis=1).astype(BF16), w_vt.astype(BF16)


def _inproj_kernel(x_ref, pos_ref, g1_ref, w_ref, wvt_ref, qg_ref, kvg_ref, wuq_ref, wuqr_ref, wk_ref, wv_ref,
                   invl_ref, qconst_ref, gexp_ref,
                   q_out, kc_out, vc_out, ks_out, vs_out, kw_out, vw_out, g3_out,
                   qm_out, km_out, vm_out, gn_out, gm_out, *, tm):
    xn = _rms(x_ref[...], g1_ref[...]).astype(BF16)

    def proj(name, c0=0, c1=None):
        a, b = _SEG_OFF[name]
        c1 = b - a if c1 is None else c1
        return _dot(xn, w_ref[:, a + c0:a + c1])

    for h in range(NSA_HEADS):
        sl = slice(h * LANE, (h + 1) * LANE)
        q_out[:, sl] = (proj('q', h * LANE, (h + 1) * LANE) + qconst_ref[:, sl]).astype(BF16)
    kc_out[...] = proj('kc').astype(BF16)
    vc_out[...] = proj('vc').astype(BF16)

    row = lax.broadcasted_iota(jnp.int32, (tm, LANE), 0)
    lane = lax.broadcasted_iota(jnp.int32, (tm, LANE), 1)
    t = (pl.program_id(0) * tm + row) & (SEQ - 1)
    blk = t >> 6
    in_pos = (lane >= POS_LANE0) & (lane < POS_LANE0 + 2 * POS_PIECES)
    pos_aug = jnp.where(in_pos, jnp.where((lane & 1) == 0, blk, t & 63), 0).astype(F32)
    sel_aug = pos_aug + jnp.where(lane == SEL_LANE0 + blk, 1.0, 0.0)
    for g in range(NSA_GROUPS):
        sl = slice(g * LANE, (g + 1) * LANE)
        ks_out[:, sl] = (proj('ks', g * LANE, (g + 1) * LANE) + sel_aug).astype(BF16)
        kw_out[:, sl] = (proj('kw', g * LANE, (g + 1) * LANE) + pos_aug).astype(BF16)
    def store_vt(out, vals):
        for g in range(vals.shape[0] // LANE):
            out[0, g * VT_ROWS:g * VT_ROWS + LANE, :] = vals[g * LANE:(g + 1) * LANE].astype(BF16)
            out[0, g * VT_ROWS + LANE:(g + 1) * VT_ROWS, :] = jnp.ones((BF16_ROWS, tm), BF16)

    store_vt(vs_out, _dot_nt(wvt_ref[0:2 * LANE, :], xn))
    store_vt(vw_out, _dot_nt(wvt_ref[2 * LANE:4 * LANE, :], xn))
    gate = jax.nn.sigmoid(proj('g3'))
    g_hi = gate.astype(BF16)
    g_lo = (gate - g_hi.astype(F32)).astype(BF16)
    for c in range(3):
        ex = gexp_ref[:, c * 512:(c + 1) * 512]
        g3_out[:, c * 512:(c + 1) * 512] = _dot(g_hi, ex) + _dot(g_lo, ex)
    for c in range(2):
        gn_out[:, c * 512:(c + 1) * 512] = jax.nn.sigmoid(proj('gn', c * 512, (c + 1) * 512))
        gm_out[:, c * 512:(c + 1) * 512] = jax.nn.sigmoid(proj('gm', c * 512, (c + 1) * 512))

    ang = pos_ref[...].astype(F32) * invl_ref[...]
    cos, sin = jnp.cos(ang), jnp.sin(ang)
    cqn = _rms(proj('cq'), qg_ref[...]).astype(BF16)
    ckvn = _rms(proj('ckv'), kvg_ref[...]).astype(BF16)
    kpe = proj('kr') * cos + proj('krr') * sin
    qscale = (MLA_NOPE + MLA_ROPE) ** -0.5 * LOG2E
    for h in range(MLA_HEADS):
        sl = slice(h * LANE, (h + 1) * LANE)
        qh = _dot(cqn, wuq_ref[:, sl]) * cos + _dot(cqn, wuqr_ref[:, sl]) * sin
        qm_out[:, sl] = (qh * qscale).astype(BF16)
        km_out[:, sl] = (_dot(ckvn, wk_ref[:, sl]) + kpe).astype(BF16)
    store_vt(vm_out, _dot_nt(wv_ref[...], ckvn))


def _inproj(x2, pos2, norm1_g, w_in, q_norm_g, kv_norm_g, w_uq, w_ukv, *, tm=256):
    n = x2.shape[0]
    w_pad, w_vt = _build_w_in(w_in)
    uq3 = w_uq.reshape(MLA_Q_LORA, MLA_HEADS, MLA_NOPE + MLA_ROPE)
    wuq = jnp.pad(uq3, ((0, 0), (0, 0), (0, LANE - MLA_NOPE - MLA_ROPE))).reshape(MLA_Q_LORA, -1).astype(BF16)
    uq_rot = jnp.concatenate([jnp.zeros_like(uq3[..., :MLA_NOPE]), _rope_rot_cols(uq3[..., MLA_NOPE:])], axis=-1)
    wuqr = jnp.pad(uq_rot, ((0, 0), (0, 0), (0, LANE - MLA_NOPE - MLA_ROPE))).reshape(MLA_Q_LORA, -1).astype(BF16)
    ukv3 = w_ukv.reshape(MLA_KV_LORA, MLA_HEADS, MLA_NOPE + MLA_V)
    wk = jnp.pad(ukv3[..., :MLA_NOPE], ((0, 0), (0, 0), (0, LANE - MLA_NOPE))).reshape(MLA_KV_LORA, -1).astype(BF16)
    wv = ukv3[..., MLA_NOPE:].reshape(MLA_KV_LORA, -1).T.astype(BF16)

    inv = ROPE_THETA ** (-np.arange(0, MLA_ROPE, 2, dtype=np.float32) / MLA_ROPE)
    invl = np.zeros((1, LANE), np.float32)
    invl[0, MLA_NOPE:MLA_NOPE + MLA_ROPE] = np.concatenate([inv, inv])
    slopes = 2.0 ** (-8.0 * np.arange(1, NSA_HEADS + 1, dtype=np.float32) / NSA_HEADS)
    qconst = np.zeros((1, NSA_HEADS, LANE), np.float32)
    rest = (slopes * np.float32(LOG2E)).astype(np.float32)
    for piece in range(POS_PIECES):
        c = rest.astype(BF16).astype(np.float32)
        qconst[0, :, POS_LANE0 + 2 * piece] = c * SEL_BLOCK
        qconst[0, :, POS_LANE0 + 2 * piece + 1] = c
        rest = (rest - c).astype(np.float32)
    qconst = qconst.reshape(1, NSA_HEADS * LANE)
    gexp = np.zeros((LANE, 3 * NSA_HEADS, HEAD_DIM), np.float32)
    gexp[np.arange(3 * NSA_HEADS), np.arange(3 * NSA_HEADS), :] = 1.0
    gexp = gexp.reshape(LANE, 3 * NSA_HEADS * HEAD_DIM)

    row = lambda w: pl.BlockSpec((tm, w), lambda i: (i, 0))
    outs = [('q', 1024, BF16), ('kc', 128, BF16), ('vc', 128, BF16), ('ks', 256, BF16), ('vs', -2 * VT_ROWS, BF16),
            ('kw', 256, BF16), ('vw', -2 * VT_ROWS, BF16), ('g3', 1536, F32), ('qm', 1024, BF16),
            ('km', 1024, BF16), ('vm', -4 * VT_ROWS, BF16), ('gn', 1024, F32), ('gm', 1024, F32)]
    ospec = lambda w: row(w) if w > 0 else pl.BlockSpec((1, -w, tm), lambda i: (i, 0, 0))
    oshape = lambda w: (n, w) if w > 0 else (n // tm, -w, tm)
    res = pl.pallas_call(
        functools.partial(_inproj_kernel, tm=tm),
        grid=(n // tm,),
        in_specs=[row(D_MODEL), row(1), _const_spec((1, D_MODEL)), _const_spec((D_MODEL, W_PAD)),
                  _const_spec(w_vt.shape), _const_spec((1, MLA_Q_LORA)), _const_spec((1, MLA_KV_LORA)),
                  _const_spec(wuq.shape), _const_spec(wuqr.shape), _const_spec(wk.shape), _const_spec(wv.shape),
                  _const_spec((1, LANE)), _const_spec((1, NSA_HEADS * LANE)), _const_spec(gexp.shape)],
        out_specs=[ospec(w) for _, w, _ in outs],
        out_shape=[jax.ShapeDtypeStruct(oshape(w), dt) for _, w, dt in outs],
        compiler_params=_cparams(("parallel",)),
    )(x2, pos2, norm1_g.reshape(1, -1), w_pad, w_vt, q_norm_g.reshape(1, -1), kv_norm_g.reshape(1, -1),
      wuq, wuqr, wk, wv, jnp.asarray(invl), jnp.asarray(qconst), jnp.asarray(gexp, BF16))
    return dict(zip([o[0] for o in outs], res))


def _compress_kernel(zk_ref, zv_ref, pk_ref, pv_ref, w1k_ref, w1v_ref, w2k_ref, w2vt_ref, aug_ref,
                     kc_out, vc_out):
    half = CMP_STRIDE * NSA_GROUPS * HEAD_DIM

    def hidden(z_ref, p_ref, w1_ref):
        z = z_ref[0]
        top = _dot(z, w1_ref[0:half, :])
        bot = _dot(z, w1_ref[half:2 * half, :])
        pos = _dot(p_ref[:, 0:half], w1_ref[0:half, :]) + _dot(p_ref[:, half:2 * half], w1_ref[half:2 * half, :])
        pre = top + pltpu.roll(bot, N_CMP - 1, 0) + pos[0:1, :]
        return jax.nn.gelu(pre).astype(BF16)

    kc_out[0] = (_dot(hidden(zk_ref, pk_ref, w1k_ref), w2k_ref[...]) + aug_ref[...]).astype(BF16)
    vc_out[0] = _dot_nt(w2vt_ref[...], hidden(zv_ref, pv_ref, w1v_ref)).astype(BF16)


def _expand_w1(w1):
    w = w1.reshape(2, CMP_STRIDE, HEAD_DIM, CMP_HIDDEN)
    eye = jnp.eye(NSA_GROUPS, dtype=w1.dtype)
    w = jnp.einsum('hldc,ge->hlgdec', w, eye)
    return w.reshape(2 * CMP_STRIDE * NSA_GROUPS * HEAD_DIM, NSA_GROUPS * CMP_HIDDEN).astype(BF16)


def _expand_pos(pos):
    p = jnp.broadcast_to(pos.reshape(2, CMP_STRIDE, 1, HEAD_DIM), (2, CMP_STRIDE, NSA_GROUPS, HEAD_DIM))
    return jnp.broadcast_to(p.reshape(1, -1), (8, 2 * CMP_STRIDE * NSA_GROUPS * HEAD_DIM)).astype(BF16)


def _expand_w2(w2, dup):
    blk = jnp.concatenate([w2, w2 if dup else jnp.zeros_like(w2)], axis=-1)
    eye = jnp.eye(NSA_GROUPS, dtype=w2.dtype)
    return jnp.einsum('cd,ge->gced', blk, eye).reshape(NSA_GROUPS * CMP_HIDDEN, NSA_GROUPS * LANE).astype(BF16)


def _compress(kc_raw, vc_raw, pos_k, w1_k, w2_k, pos_v, w1_v, w2_v, batch):
    zk = kc_raw.reshape(batch, N_CMP, CMP_STRIDE * LANE)
    zv = vc_raw.reshape(batch, N_CMP, CMP_STRIDE * LANE)
    endpos = np.arange(N_CMP) * CMP_STRIDE + CMP_BLOCK - 1
    aug = np.zeros((N_CMP, NSA_GROUPS, LANE), np.float32)
    for piece in range(POS_PIECES):
        aug[:, :, POS_LANE0 + 2 * piece] = (endpos // SEL_BLOCK)[:, None]
        aug[:, :, POS_LANE0 + 2 * piece + 1] = (endpos % SEL_BLOCK)[:, None]
    aug = aug.reshape(N_CMP, NSA_GROUPS * LANE)
    zspec = pl.BlockSpec((1, N_CMP, CMP_STRIDE * LANE), lambda b: (b, 0, 0))
    ospec = pl.BlockSpec((1, N_CMP, NSA_GROUPS * LANE), lambda b: (b, 0, 0))
    args = (zk, zv, _expand_pos(pos_k), _expand_pos(pos_v), _expand_w1(w1_k), _expand_w1(w1_v),
            _expand_w2(w2_k, False), _expand_w2(w2_v, True).T, jnp.asarray(aug))
    return pl.pallas_call(
        _compress_kernel,
        grid=(batch,),
        in_specs=[zspec, zspec] + [_const_spec(a.shape) for a in args[2:]],
        out_specs=[ospec, pl.BlockSpec((1, NSA_GROUPS * LANE, N_CMP), lambda b: (b, 0, 0))],
        out_shape=[jax.ShapeDtypeStruct((batch, N_CMP, NSA_GROUPS * LANE), BF16),
                   jax.ShapeDtypeStruct((batch, NSA_GROUPS * LANE, N_CMP), BF16)],
        compiler_params=_cparams(("parallel",)),
    )(*args)


def _nsa_cmp_kernel(q_ref, kc_ref, vct_ref, gate_ref, ovt_ref, ocmp_out, selb_out, *, tq):
    qi = pl.program_id(1)
    slot = lax.broadcasted_iota(jnp.int32, (N_CMP, tq), 0)
    tslot = qi * tq + lax.broadcasted_iota(jnp.int32, (N_CMP, tq), 1)
    visible = tslot >= slot * CMP_STRIDE + (CMP_BLOCK - 1)
    sblk = lax.broadcasted_iota(jnp.int32, (N_SEL, tq), 0)
    back = ((qi * tq + lax.broadcasted_iota(jnp.int32, (N_SEL, tq), 1)) >> 6) - sblk
    causal = back >= 0
    forced = causal & ((back < N_LOCAL_FORCED) | (sblk == 0))
    low_rows = lax.broadcasted_iota(jnp.int32, (LANE, tq), 0) < HEAD_DIM

    for g in range(NSA_GROUPS):
        kc = kc_ref[0, :, g * LANE:(g + 1) * LANE]
        vct = vct_ref[0, g * LANE:(g + 1) * LANE, :]
        imp = jnp.zeros((N_SEL, tq), F32)
        outs = []
        for hh in range(NSA_HPG):
            h = g * NSA_HPG + hh
            s = jnp.where(visible, _dot_nt(kc, q_ref[:, h * LANE:(h + 1) * LANE]), NEG)
            m = jnp.max(s, axis=0, keepdims=True)
            p = jnp.where(visible, jnp.exp2(s - m), 0.0)
            p = p / jnp.maximum(jnp.sum(p, axis=0, keepdims=True), 1e-30)
            pb = p.astype(BF16)
            outs.append(_dot(vct, pb))
            imp = imp + _dot(ovt_ref[...], pb)
        for pr in range(NSA_HPG // 2):
            c0 = (g * 2 + pr) * LANE
            o = jnp.where(low_rows, outs[2 * pr], outs[2 * pr + 1]).T
            ocmp_out[:, c0:c0 + LANE] = o * gate_ref[:, c0:c0 + LANE]

        score = jnp.where(causal, imp + jnp.where(forced, FORCE_BONUS, 0.0), NEG)
        rank = jnp.zeros((N_SEL, tq), jnp.int32)
        for s2 in range(N_SEL):
            other = score[s2:s2 + 1]
            ahead = (other > score) | ((other == score) & (sblk > s2))
            rank = rank + ahead.astype(jnp.int32)
        bias = jnp.where(causal & (rank < SEL_TOPK), 0.0, NEG)
        placed = jnp.concatenate([jnp.zeros((SEL_LANE0, tq), F32), bias,
                                  jnp.zeros((LANE - SEL_LANE0 - N_SEL, tq), F32)], axis=0)
        selb_out[:, g * LANE:(g + 1) * LANE] = placed.T.astype(BF16)


def _nsa_cmp(q, kc, vct, g3, batch, *, tq=256):
    n = q.shape[0]
    nq = SEQ // tq
    cmp_start = np.arange(N_CMP) * CMP_STRIDE
    sel_start = np.arange(N_SEL) * SEL_BLOCK
    ov = np.clip(np.minimum(cmp_start[:, None] + CMP_BLOCK, sel_start[None, :] + SEL_BLOCK)
                 - np.maximum(cmp_start[:, None], sel_start[None, :]), 0, None).astype(np.float32) / CMP_BLOCK
    rows = lambda w: pl.BlockSpec((tq, w), lambda b, i: (b * nq + i, 0))
    return pl.pallas_call(
        functools.partial(_nsa_cmp_kernel, tq=tq),
        grid=(batch, nq),
        in_specs=[rows(1024), pl.BlockSpec((1, N_CMP, NSA_GROUPS * LANE), lambda b, i: (b, 0, 0)),
                  pl.BlockSpec((1, NSA_GROUPS * LANE, N_CMP), lambda b, i: (b, 0, 0)), rows(512),
                  _const_spec((N_SEL, N_CMP))],
        out_specs=[rows(512), rows(NSA_GROUPS * LANE)],
        out_shape=[jax.ShapeDtypeStruct((n, 512), F32), jax.ShapeDtypeStruct((n, NSA_GROUPS * LANE), BF16)],
        compiler_params=_cparams(("parallel", "parallel")),
    )(q, kc, vct, g3, jnp.asarray(ov.T, BF16))


def _flash_init(m_ref, l_ref, acc_ref):
    m_ref[...] = jnp.full(m_ref.shape, NEG, F32)
    l_ref[...] = jnp.zeros(l_ref.shape, F32)
    acc_ref[...] = jnp.zeros(acc_ref.shape, F32)


def _flash_update(st, vt, m_ref, l_ref, acc_ref):
    m_old = m_ref[...]
    m_new = jnp.maximum(m_old, jnp.max(st, axis=0, keepdims=True))
    alpha = jnp.exp2(m_old - m_new)
    pv = _dot(vt, jnp.exp2(st - m_new).astype(BF16))
    l_ref[...] = alpha * l_ref[...] + pv[LANE:LANE + 1]
    acc_ref[...] = alpha * acc_ref[...] + pv[0:LANE]
    m_ref[...] = m_new


def _tile_iotas(heads, t):
    key = lax.broadcasted_iota(jnp.int32, (t, heads * t), 0)
    qry = lax.broadcasted_iota(jnp.int32, (t, heads * t), 1) & (t - 1)
    return key, qry


def _nsa_attn_kernel(q_ref, selb_ref, ks_ref, vs_ref, kw_ref, vw_ref, gs_ref, gw_ref, ocmp_ref, o_out,
                     m_ref, l_ref, acc_ref, *, t):
    qi = pl.program_id(2)
    key, qry = _tile_iotas(NSA_HPG, t)
    qs = [q_ref[:, hh * LANE:(hh + 1) * LANE] for hh in range(NSA_HPG)]
    q_win = jnp.concatenate(qs, axis=0)
    q_sel = jnp.concatenate([q + selb_ref[...] for q in qs], axis=0)
    low_rows = lax.broadcasted_iota(jnp.int32, (LANE, t), 0) < HEAD_DIM

    def keys(ref, j):
        return ref[pl.ds(pl.multiple_of(j * t, t), t), :]

    def finish():
        o = acc_ref[...] / l_ref[...]
        return [jnp.where(low_rows, o[:, (2 * p) * t:(2 * p + 1) * t], o[:, (2 * p + 1) * t:(2 * p + 2) * t]).T
                for p in range(NSA_HPG // 2)]

    _flash_init(m_ref, l_ref, acc_ref)
    s = _dot_nt(keys(ks_ref, qi), q_sel)
    _flash_update(jnp.where(key <= qry, s, NEG), vs_ref[qi], m_ref, l_ref, acc_ref)

    def sel_body(j, carry):
        _flash_update(_dot_nt(keys(ks_ref, j), q_sel), vs_ref[j], m_ref, l_ref, acc_ref)
        return carry

    lax.fori_loop(0, qi, sel_body, 0)
    o_sel = finish()

    _flash_init(m_ref, l_ref, acc_ref)
    s = _dot_nt(keys(kw_ref, qi), q_win)
    _flash_update(jnp.where(key <= qry, s, NEG), vw_ref[qi], m_ref, l_ref, acc_ref)

    @pl.when(qi >= 1)
    def _():
        _flash_update(_dot_nt(keys(kw_ref, qi - 1), q_win), vw_ref[qi - 1], m_ref, l_ref, acc_ref)

    @pl.when(qi >= 2)
    def _():
        s = _dot_nt(keys(kw_ref, qi - 2), q_win)
        _flash_update(jnp.where(key > qry, s, NEG), vw_ref[qi - 2], m_ref, l_ref, acc_ref)

    o_win = finish()
    for p in range(NSA_HPG // 2):
        sl = slice(p * LANE, (p + 1) * LANE)
        o_out[:, sl] = (gs_ref[:, sl] * o_sel[p] + gw_ref[:, sl] * o_win[p] + ocmp_ref[:, sl]).astype(BF16)


def _nsa_attn(q, selb, ks, vs, kw, vw, g3, ocmp, batch, *, t=256):
    assert WINDOW == 2 * t
    n = q.shape[0]
    nq = SEQ // t
    gw = NSA_HPG * HEAD_DIM
    rows = lambda w, off=0: pl.BlockSpec((t, w), lambda b, g, i: (b * nq + i, g + off))
    seq = pl.BlockSpec((SEQ, LANE), lambda b, g, i: (b, g))
    seq_t = pl.BlockSpec((nq, VT_ROWS, t), lambda b, g, i: (b, g, 0))
    m = NSA_HPG * t
    return pl.pallas_call(
        functools.partial(_nsa_attn_kernel, t=t),
        grid=(batch, NSA_GROUPS, nq),
        in_specs=[rows(NSA_HPG * LANE), rows(LANE), seq, seq_t, seq, seq_t,
                  rows(gw, 2), rows(gw, 4), rows(gw)],
        out_specs=rows(gw),
        out_shape=jax.ShapeDtypeStruct((n, NSA_HEADS * HEAD_DIM), BF16),
        scratch_shapes=[pltpu.VMEM((1, m), F32), pltpu.VMEM((1, m), F32), pltpu.VMEM((LANE, m), F32)],
        compiler_params=_cparams(("parallel", "parallel", "arbitrary")),
    )(q, selb, ks, vs, kw, vw, g3, g3, ocmp)


def _mla_attn_kernel(q_ref, k_ref, v_ref, o_out, m_ref, l_ref, acc_ref, *, tq, tk):
    qi = pl.program_id(2)
    per = tq // tk
    key = lax.broadcasted_iota(jnp.int32, (tk, 2 * tq), 0)
    qry = lax.broadcasted_iota(jnp.int32, (tk, 2 * tq), 1) & (tq - 1)
    q0, q1 = q_ref[:, 0:LANE], q_ref[:, LANE:2 * LANE]

    def scores(j):
        k = k_ref[pl.ds(pl.multiple_of(j * tk, tk), tk), :]
        return jnp.concatenate([_dot_nt(k[:, 0:LANE], q0), _dot_nt(k[:, LANE:2 * LANE], q1)], axis=1)

    _flash_init(m_ref, l_ref, acc_ref)
    for d in range(per):
        j = qi * per + d
        _flash_update(jnp.where(key + d * tk <= qry, scores(j), NEG), v_ref[j], m_ref, l_ref, acc_ref)

    def body(j, carry):
        _flash_update(scores(j), v_ref[j], m_ref, l_ref, acc_ref)
        return carry

    lax.fori_loop(0, qi * per, body, 0)
    o = acc_ref[...] / l_ref[...]
    low_rows = lax.broadcasted_iota(jnp.int32, (LANE, tq), 0) < MLA_V
    o_out[...] = jnp.where(low_rows, o[:, 0:tq], o[:, tq:2 * tq]).T.astype(BF16)


def _mla_attn(qm, km, vm, batch, *, tq=512):
    n = qm.shape[0]
    nk, _, tk = vm.shape
    nq = SEQ // tq
    return pl.pallas_call(
        functools.partial(_mla_attn_kernel, tq=tq, tk=tk),
        grid=(batch, MLA_HEADS // 2, nq),
        in_specs=[pl.BlockSpec((tq, 2 * LANE), lambda b, p, i: (b * nq + i, p)),
                  pl.BlockSpec((SEQ, 2 * LANE), lambda b, p, i: (b, p)),
                  pl.BlockSpec((SEQ // tk, VT_ROWS, tk), lambda b, p, i: (b, p, 0))],
        out_specs=pl.BlockSpec((tq, LANE), lambda b, p, i: (b * nq + i, p)),
        out_shape=jax.ShapeDtypeStruct((n, MLA_HEADS * MLA_V), BF16),
        scratch_shapes=[pltpu.VMEM((1, 2 * tq), F32), pltpu.VMEM((1, 2 * tq), F32),
                        pltpu.VMEM((LANE, 2 * tq), F32)],
        compiler_params=_cparams(("parallel", "parallel", "arbitrary")),
    )(qm, km, vm)


def _mix_kernel(x_ref, on_ref, om_ref, gn_ref, gm_ref, wn_ref, wm_ref, wo_ref, g2_ref, wq_ref, k1_ref, k2_ref,
                h_out, xn_out, s_out):
    merged = gn_ref[...] * _dot(on_ref[...], wn_ref[...]) + gm_ref[...] * _dot(om_ref[...], wm_ref[...])
    h = x_ref[...] + _dot(merged.astype(BF16), wo_ref[...])
    h_out[...] = h
    xn32 = _rms(h, g2_ref[...])
    xn = xn32.astype(BF16)
    xn_out[...] = xn32.T.astype(BF16)
    half = PEER_QDIM // 2
    for hd in range(PEER_HEADS):
        q = _dot(xn, wq_ref[:, hd * PEER_QDIM:(hd + 1) * PEER_QDIM]).astype(BF16)
        s_out[2 * hd] = _dot_nt(k1_ref[...], q[:, 0:half])
        s_out[2 * hd + 1] = _dot_nt(k2_ref[...], q[:, half:PEER_QDIM])


def _mix(x2, o_nsa, o_mla, gn, gm, w_n, w_m, w_o, norm2_g, w_q, keys1, keys2, *, tm=256):
    n = x2.shape[0]
    row = lambda w: pl.BlockSpec((tm, w), lambda i: (i, 0))
    ws = [w_n.astype(BF16), w_m.astype(BF16), w_o.astype(BF16), norm2_g.reshape(1, -1), w_q.astype(BF16),
          keys1.astype(BF16), keys2.astype(BF16)]
    return pl.pallas_call(
        _mix_kernel,
        grid=(n // tm,),
        in_specs=[row(D_MODEL), row(512), row(512), row(D_MODEL), row(D_MODEL)] + [_const_spec(w.shape) for w in ws],
        out_specs=[row(D_MODEL), pl.BlockSpec((D_MODEL, tm), lambda i: (0, i)),
                   pl.BlockSpec((2 * PEER_HEADS, PEER_KEYS, tm), lambda i: (0, 0, i))],
        out_shape=[jax.ShapeDtypeStruct((n, D_MODEL), F32), jax.ShapeDtypeStruct((D_MODEL, n), BF16),
                   jax.ShapeDtypeStruct((2 * PEER_HEADS, PEER_KEYS, n), F32)],
        compiler_params=_cparams(("parallel",)),
    )(x2, o_nsa, o_mla, gn, gm, *ws)


_CAND_ROWS = 16 + 7 * 8 + 8


def _take_top(v, k, idx, exact):
    n = v.shape[0]
    rank = jnp.full(v.shape, k, jnp.int32)
    vals = []
    for r in range(k):
        m = jnp.max(v, axis=0, keepdims=True)
        hit = v == m
        if exact:
            hit = idx == jnp.min(jnp.where(hit, idx, n), axis=0, keepdims=True)
        rank = jnp.where(hit, r, rank)
        v = jnp.where(hit, -jnp.inf, v)
        vals.append(m)
    return rank, vals


def _route(s1, s2, exact):
    k = PEER_TOPK
    tl = s1.shape[1]
    kidx = lax.broadcasted_iota(jnp.int32, (PEER_KEYS, tl), 0)
    rank1, a = _take_top(s1, k, kidx, exact)
    rank2, b = _take_top(s2, k, kidx, exact)
    bcat = jnp.concatenate(b, axis=0)
    groups = [a[0] + bcat]
    for p in range(1, 8):
        groups.append(a[p] + bcat[0:8])
    groups.append(jnp.concatenate(a[8:16], axis=0) + b[0])
    cand = jnp.concatenate(groups, axis=0)
    ridx = lax.broadcasted_iota(jnp.int32, (_CAND_ROWS, tl), 0)
    qpos = jnp.where(ridx < 16, ridx, (ridx - 16) & 7)
    ppos = jnp.where(ridx < 16, 0, ((ridx - 16) >> 3) + 1)
    valid = (ridx >= 16 + 7 * 8) | ((ppos + 1) * (qpos + 1) <= k)
    crank, _ = _take_top(jnp.where(valid, cand, -jnp.inf), k, ridx, exact)
    taken = crank < k
    takenf = taken.astype(F32)
    z = jnp.sum(jnp.where(taken, jnp.exp(cand - cand[0:1]), 0.0), axis=0, keepdims=True)
    cnt = jnp.zeros((PEER_KEYS, tl), F32)
    for p in range(k):
        if p == 0:
            c = jnp.sum(takenf[0:16], axis=0, keepdims=True)
        elif p < 8:
            c = jnp.sum(takenf[16 + (p - 1) * 8:16 + p * 8], axis=0, keepdims=True)
        else:
            c = takenf[72 + (p - 8):72 + (p - 8) + 1]
        cnt = jnp.where(rank1 == p, c, cnt)
    ea = jnp.where(rank1 < k, jnp.exp(s1 - a[0]), 0.0) / z
    eb = jnp.where(rank2 < k, jnp.exp(s2 - b[0]), 0.0)
    n_taken = (jnp.sum((rank1 < k).astype(F32), axis=0, keepdims=True)
               + jnp.sum((rank2 < k).astype(F32), axis=0, keepdims=True)
               + jnp.sum(takenf, axis=0, keepdims=True))
    return (cnt, ea, rank2.astype(F32).astype(BF16), eb.astype(BF16)), jnp.max(n_taken)


def _peer_topk_kernel(s_ref, cnt_out, ea_out, r2_out, eb_out):
    s1, s2 = s_ref[0], s_ref[1]
    outs = (cnt_out, ea_out, r2_out, eb_out)
    res, n_taken = _route(s1, s2, exact=False)
    for o, r in zip(outs, res):
        o[0] = r

    @pl.when(n_taken > 3 * PEER_TOPK)
    def _():
        res, _ = _route(s1, s2, exact=True)
        for o, r in zip(outs, res):
            o[0] = r


def _peer_topk(scores, *, tl=512):
    n = scores.shape[-1]
    ospec = pl.BlockSpec((1, PEER_KEYS, tl), lambda i, h: (h, 0, i))
    shp = (PEER_HEADS, PEER_KEYS, n)
    return pl.pallas_call(
        _peer_topk_kernel,
        grid=(n // tl, PEER_HEADS),
        in_specs=[pl.BlockSpec((2, PEER_KEYS, tl), lambda i, h: (h, 0, i))],
        out_specs=[ospec] * 4,
        out_shape=[jax.ShapeDtypeStruct(shp, F32), jax.ShapeDtypeStruct(shp, F32),
                   jax.ShapeDtypeStruct(shp, BF16), jax.ShapeDtypeStruct(shp, BF16)],
        compiler_params=_cparams(("parallel", "parallel")),
    )(scores)


def _peer_dense_kernel(xnt_ref, d_ref, ut_ref, cnt_ref, ea_ref, r2_ref, eb_ref, y_out, g_ref, *, rows_per_step):
    e = pl.program_id(1)
    tl = xnt_ref.shape[1]
    chunk = 2 * PEER_KEYS

    @pl.when(e == 0)
    def _():
        y_out[...] = jnp.zeros(y_out.shape, F32)

    for c in range(rows_per_step // 2):
        hid = _dot(d_ref[c * chunk:(c + 1) * chunk, :], xnt_ref[...])
        for k2 in range(2):
            ii = 2 * c + k2
            w = None
            for hd in range(PEER_HEADS):
                cnt = jnp.broadcast_to(cnt_ref[hd, ii:ii + 1, :], (BF16_ROWS, tl)).astype(BF16)
                ea = jnp.broadcast_to(ea_ref[hd, ii:ii + 1, :], (BF16_ROWS, tl)).astype(BF16)
                wh = jnp.where(r2_ref[hd] < cnt[None], eb_ref[hd] * ea[None], jnp.zeros((), BF16))
                w = wh if w is None else w + wh
            act = _gelu_tanh(hid[k2 * PEER_KEYS:(k2 + 1) * PEER_KEYS, :]).astype(BF16)
            g_ref[ii * PEER_KEYS:(ii + 1) * PEER_KEYS, :] = act * w.reshape(PEER_KEYS, tl)
    y_out[...] += _dot(ut_ref[...], g_ref[...])


def _peer_dense(xnt, d_bf, ut_bf, cnt, ea, r2, eb, *, tl=1024, eb_rows=8):
    n = xnt.shape[1]
    ebk = eb_rows * PEER_KEYS
    groups = PEER_KEYS // BF16_ROWS
    split = lambda a: a.reshape(PEER_HEADS, groups, BF16_ROWS, n)
    full = pl.BlockSpec((PEER_HEADS, groups, BF16_ROWS, tl), lambda i, e: (0, 0, 0, i))
    part = pl.BlockSpec((PEER_HEADS, eb_rows, tl), lambda i, e: (0, e, i))
    return pl.pallas_call(
        functools.partial(_peer_dense_kernel, rows_per_step=eb_rows),
        grid=(n // tl, PEER_EXPERTS // ebk),
        in_specs=[pl.BlockSpec((D_MODEL, tl), lambda i, e: (0, i)),
                  pl.BlockSpec((ebk, D_MODEL), lambda i, e: (e, 0)),
                  pl.BlockSpec((D_MODEL, ebk), lambda i, e: (0, e)),
                  part, part, full, full],
        out_specs=pl.BlockSpec((D_MODEL, tl), lambda i, e: (0, i)),
        out_shape=jax.ShapeDtypeStruct((D_MODEL, n), F32),
        scratch_shapes=[pltpu.VMEM((ebk, tl), BF16)],
        compiler_params=_cparams(("parallel", "arbitrary")),
    )(xnt, d_bf, ut_bf, cnt, ea, split(r2), split(eb))


def _final_kernel(h_ref, yt_ref, g_ref, o_out):
    o_out[...] = _rms(h_ref[...] + yt_ref[...].T, g_ref[...])


def _final(h, yt, norm_f_g, *, tm=256):
    n = h.shape[0]
    return pl.pallas_call(
        _final_kernel,
        grid=(n // tm,),
        in_specs=[pl.BlockSpec((tm, D_MODEL), lambda i: (i, 0)), pl.BlockSpec((D_MODEL, tm), lambda i: (0, i)),
                  _const_spec((1, D_MODEL))],
        out_specs=pl.BlockSpec((tm, D_MODEL), lambda i: (i, 0)),
        out_shape=jax.ShapeDtypeStruct((n, D_MODEL), F32),
        compiler_params=_cparams(("parallel",)),
    )(h, yt, norm_f_g.reshape(1, -1))


def _layer(h2, pos2, batch, p):
    a = _inproj(h2, pos2, p['norm1_g'], p['w_in'], p['mla_q_norm_g'], p['mla_kv_norm_g'], p['mla_w_uq'],
                p['mla_w_ukv'])
    kc, vc = _compress(a['kc'], a['vc'], p['nsa_cmp_pos_k'], p['nsa_cmp_w1_k'], p['nsa_cmp_w2_k'],
                       p['nsa_cmp_pos_v'], p['nsa_cmp_w1_v'], p['nsa_cmp_w2_v'], batch)
    ocmp, selb = _nsa_cmp(a['q'], kc, vc, a['g3'], batch)
    o_nsa = _nsa_attn(a['q'], selb, a['ks'], a['vs'], a['kw'], a['vw'], a['g3'], ocmp, batch)
    o_mla = _mla_attn(a['qm'], a['km'], a['vm'], batch)
    h_mid, xn, scores = _mix(h2, o_nsa, o_mla, a['gn'], a['gm'], p['w_branch_nsa'], p['w_branch_mla'], p['w_out'],
                             p['norm2_g'], p['peer_w_query'], p['peer_sub_keys_1'], p['peer_sub_keys_2'])
    cnt, ea, r2, eb = _peer_topk(scores)
    yt = _peer_dense(xn, p['peer_expert_down'].astype(BF16), p['peer_expert_up'].T.astype(BF16), cnt, ea, r2, eb)
    return h_mid, yt


def kernel(x, positions, norm1_g, w_in, nsa_cmp_pos_k, nsa_cmp_w1_k, nsa_cmp_w2_k, nsa_cmp_pos_v, nsa_cmp_w1_v,
           nsa_cmp_w2_v, mla_q_norm_g, mla_kv_norm_g, mla_w_uq, mla_w_ukv, w_branch_nsa, w_branch_mla, w_out,
           norm2_g, peer_w_query, peer_sub_keys_1, peer_sub_keys_2, peer_expert_down, peer_expert_up, norm_f_g):
    batch, seq, d = x.shape
    assert seq == SEQ and d == D_MODEL
    stacked = dict(norm1_g=norm1_g, w_in=w_in, nsa_cmp_pos_k=nsa_cmp_pos_k, nsa_cmp_w1_k=nsa_cmp_w1_k,
                   nsa_cmp_w2_k=nsa_cmp_w2_k, nsa_cmp_pos_v=nsa_cmp_pos_v, nsa_cmp_w1_v=nsa_cmp_w1_v,
                   nsa_cmp_w2_v=nsa_cmp_w2_v, mla_q_norm_g=mla_q_norm_g, mla_kv_norm_g=mla_kv_norm_g,
                   mla_w_uq=mla_w_uq, mla_w_ukv=mla_w_ukv, w_branch_nsa=w_branch_nsa, w_branch_mla=w_branch_mla,
                   w_out=w_out, norm2_g=norm2_g, peer_w_query=peer_w_query, peer_sub_keys_1=peer_sub_keys_1,
                   peer_sub_keys_2=peer_sub_keys_2, peer_expert_down=peer_expert_down,
                   peer_expert_up=peer_expert_up)
    assert w_in.shape[0] == 1, "single-layer block"
    p = {k: v[0] for k, v in stacked.items()}
    h_mid, yt = _layer(x.reshape(batch * seq, d), positions.reshape(batch * seq, 1), batch, p)
    return _final(h_mid, yt, norm_f_g).reshape(batch, seq, d)
```

```python
import functools
import math

import numpy as np
import jax
import jax.numpy as jnp
from jax import lax
from jax.experimental import pallas as pl
from jax.experimental.pallas import tpu as pltpu

F32 = jnp.float32
BF16 = jnp.bfloat16

D_MODEL = 1024
SEQ = 2048
EPS = 1e-6
NEG = -1e30
FORCE_BONUS = 1e4

NSA_HEADS = 8
NSA_GROUPS = 2
NSA_HPG = 4
HEAD_DIM = 64
CMP_BLOCK = 32
CMP_STRIDE = 16
CMP_HIDDEN = 128
N_CMP = SEQ // CMP_STRIDE
SEL_BLOCK = 64
N_SEL = SEQ // SEL_BLOCK
SEL_TOPK = 16
N_LOCAL_FORCED = 2
WINDOW = 512

MLA_HEADS = 8
MLA_Q_LORA = 256
MLA_KV_LORA = 128
MLA_NOPE = 64
MLA_ROPE = 32
MLA_V = 64
ROPE_THETA = 10000.0

PEER_HEADS = 8
PEER_KEYS = 128
PEER_EXPERTS = PEER_KEYS * PEER_KEYS
PEER_QDIM = 256
PEER_TOPK = 16

LANE = 128
BF16_ROWS = 16
VT_ROWS = LANE + BF16_ROWS
SEL_LANE0 = 64
POS_LANE0 = 96
POS_PIECES = 3
LOG2E = math.log2(math.e)

VMEM_LIMIT = 56 * 1024 * 1024

IN_SPLITS = (512, 128, 128, 128, 128, 128, 128, 24, 256, 128, 32, 1024, 1024)


def _cparams(sem, flags=None):
    return pltpu.CompilerParams(dimension_semantics=sem, vmem_limit_bytes=VMEM_LIMIT, flags=flags)


def _dot(a, b):
    return jnp.dot(a, b, preferred_element_type=F32)


def _dot_nt(a, b):
    return lax.dot_general(a, b, (((1,), (1,)), ((), ())), preferred_element_type=F32)


def _rms(x, g):
    return x * lax.rsqrt(jnp.mean(x * x, axis=-1, keepdims=True) + EPS) * g


_GELU_K1 = -2.0 * math.sqrt(2.0 / math.pi) * math.log2(math.e)
_GELU_K2 = _GELU_K1 * 0.044715


def _gelu_tanh(x):
    return x / (1.0 + jnp.exp2(x * (_GELU_K1 + _GELU_K2 * (x * x))))


def _const_spec(shape):
    nd = len(shape)
    return pl.BlockSpec(shape, lambda *_: (0,) * nd)


_SEG = dict(q=1024, kc=128, vc=128, ks=256, kw=256, g3=128,
            cq=256, ckv=128, kr=128, krr=128, gn=1024, gm=1024)
_SEG_OFF = {}
_o = 0
for _k, _w in _SEG.items():
    _SEG_OFF[_k] = (_o, _o + _w)
    _o += _w
W_PAD = _o


def _pad_heads(w, n_heads, width):
    k = w.shape[0]
    return jnp.pad(w.reshape(k, n_heads, width), ((0, 0), (0, 0), (0, LANE - width))).reshape(k, n_heads * LANE)


def _dup_heads(w, n_heads, width):
    k = w.shape[0]
    w3 = w.reshape(k, n_heads, width)
    return jnp.concatenate([w3, w3], axis=-1).reshape(k, n_heads * 2 * width)


def _rope_rot_cols(w_pe):
    half = MLA_ROPE // 2
    return jnp.concatenate([-w_pe[..., half:], w_pe[..., :half]], axis=-1)


def _build_w_in(w_in):
    offs = np.cumsum((0,) + IN_SPLITS)
    seg = [w_in[:, offs[i]:offs[i + 1]] for i in range(len(IN_SPLITS))]
    wq, wkc, wvc, wks, wvs, wkw, wvw, wg, wcq, wckv, wkr, wgn, wgm = seg
    k = w_in.shape[0]
    g3 = jnp.pad(wg, ((0, 0), (0, LANE - 3 * NSA_HEADS)))
    kr = jnp.pad(wkr, ((0, 0), (SEL_LANE0, LANE - SEL_LANE0 - MLA_ROPE)))
    krr = jnp.pad(_rope_rot_cols(wkr), ((0, 0), (SEL_LANE0, LANE - SEL_LANE0 - MLA_ROPE)))
    cols = [_pad_heads(wq * (HEAD_DIM ** -0.5 * LOG2E), NSA_HEADS, HEAD_DIM), wkc, wvc,
            _pad_heads(wks, NSA_GROUPS, HEAD_DIM), _pad_heads(wkw, NSA_GROUPS, HEAD_DIM),
            g3, wcq, wckv, kr, krr, wgn, wgm]
    w_vt = jnp.concatenate([_dup_heads(wvs, NSA_GROUPS, HEAD_DIM), _dup_heads(wvw, NSA_GROUPS, HEAD_DIM)], axis=1).T
    return jnp.concatenate(cols, axis=1).astype(BF16), w_vt.astype(BF16)


def _inproj_kernel(x_ref, pos_ref, g1_ref, w_ref, wvt_ref, qg_ref, kvg_ref, wuq_ref, wuqr_ref, wk_ref, wv_ref,
                   invl_ref, qconst_ref, gexp_ref,
                   q_out, kc_out, vc_out, ks_out, vs_out, kw_out, vw_out, g3_out,
                   qm_out, km_out, vm_out, gn_out, gm_out, *, tm):
    xn = _rms(x_ref[...], g1_ref[...]).astype(BF16)

    def proj(name, c0=0, c1=None):
        a, b = _SEG_OFF[name]
        c1 = b - a if c1 is None else c1
        return _dot(xn, w_ref[:, a + c0:a + c1])

    for h in range(NSA_HEADS):
        sl = slice(h * LANE, (h + 1) * LANE)
        q_out[:, sl] = (proj('q', h * LANE, (h + 1) * LANE) + qconst_ref[:, sl]).astype(BF16)
    kc_out[...] = proj('kc').astype(BF16)
    vc_out[...] = proj('vc').astype(BF16)

    row = lax.broadcasted_iota(jnp.int32, (tm, LANE), 0)
    lane = lax.broadcasted_iota(jnp.int32, (tm, LANE), 1)
    t = (pl.program_id(0) * tm + row) & (SEQ - 1)
    blk = t >> 6
    in_pos = (lane >= POS_LANE0) & (lane < POS_LANE0 + 2 * POS_PIECES)
    pos_aug = jnp.where(in_pos, jnp.where((lane & 1) == 0, blk, t & 63), 0).astype(F32)
    sel_aug = pos_aug + jnp.where(lane == SEL_LANE0 + blk, 1.0, 0.0)
    for g in range(NSA_GROUPS):
        sl = slice(g * LANE, (g + 1) * LANE)
        ks_out[:, sl] = (proj('ks', g * LANE, (g + 1) * LANE) + sel_aug).astype(BF16)
        kw_out[:, sl] = (proj('kw', g * LANE, (g + 1) * LANE) + pos_aug).astype(BF16)
    def store_vt(out, vals):
        for g in range(vals.shape[0] // LANE):
            out[0, g * VT_ROWS:g * VT_ROWS + LANE, :] = vals[g * LANE:(g + 1) * LANE].astype(BF16)
            out[0, g * VT_ROWS + LANE:(g + 1) * VT_ROWS, :] = jnp.ones((BF16_ROWS, tm), BF16)

    store_vt(vs_out, _dot_nt(wvt_ref[0:2 * LANE, :], xn))
    store_vt(vw_out, _dot_nt(wvt_ref[2 * LANE:4 * LANE, :], xn))
    gate = jax.nn.sigmoid(proj('g3'))
    g_hi = gate.astype(BF16)
    g_lo = (gate - g_hi.astype(F32)).astype(BF16)
    for c in range(3):
        ex = gexp_ref[:, c * 512:(c + 1) * 512]
        g3_out[:, c * 512:(c + 1) * 512] = _dot(g_hi, ex) + _dot(g_lo, ex)
    for c in range(2):
        gn_out[:, c * 512:(c + 1) * 512] = jax.nn.sigmoid(proj('gn', c * 512, (c + 1) * 512))
        gm_out[:, c * 512:(c + 1) * 512] = jax.nn.sigmoid(proj('gm', c * 512, (c + 1) * 512))

    ang = pos_ref[...].astype(F32) * invl_ref[...]
    cos, sin = jnp.cos(ang), jnp.sin(ang)
    cqn = _rms(proj('cq'), qg_ref[...]).astype(BF16)
    ckvn = _rms(proj('ckv'), kvg_ref[...]).astype(BF16)
    kpe = proj('kr') * cos + proj('krr') * sin
    qscale = (MLA_NOPE + MLA_ROPE) ** -0.5 * LOG2E
    for h in range(MLA_HEADS):
        sl = slice(h * LANE, (h + 1) * LANE)
        qh = _dot(cqn, wuq_ref[:, sl]) * cos + _dot(cqn, wuqr_ref[:, sl]) * sin
        qm_out[:, sl] = (qh * qscale).astype(BF16)
        km_out[:, sl] = (_dot(ckvn, wk_ref[:, sl]) + kpe).astype(BF16)
    store_vt(vm_out, _dot_nt(wv_ref[...], ckvn))


def _inproj(x2, pos2, norm1_g, w_in, q_norm_g, kv_norm_g, w_uq, w_ukv, *, tm=256):
    n = x2.shape[0]
    w_pad, w_vt = _build_w_in(w_in)
    uq3 = w_uq.reshape(MLA_Q_LORA, MLA_HEADS, MLA_NOPE + MLA_ROPE)
    wuq = jnp.pad(uq3, ((0, 0), (0, 0), (0, LANE - MLA_NOPE - MLA_ROPE))).reshape(MLA_Q_LORA, -1).astype(BF16)
    uq_rot = jnp.concatenate([jnp.zeros_like(uq3[..., :MLA_NOPE]), _rope_rot_cols(uq3[..., MLA_NOPE:])], axis=-1)
    wuqr = jnp.pad(uq_rot, ((0, 0), (0, 0), (0, LANE - MLA_NOPE - MLA_ROPE))).reshape(MLA_Q_LORA, -1).astype(BF16)
    ukv3 = w_ukv.reshape(MLA_KV_LORA, MLA_HEADS, MLA_NOPE + MLA_V)
    wk = jnp.pad(ukv3[..., :MLA_NOPE], ((0, 0), (0, 0), (0, LANE - MLA_NOPE))).reshape(MLA_KV_LORA, -1).astype(BF16)
    wv = ukv3[..., MLA_NOPE:].reshape(MLA_KV_LORA, -1).T.astype(BF16)

    inv = ROPE_THETA ** (-np.arange(0, MLA_ROPE, 2, dtype=np.float32) / MLA_ROPE)
    invl = np.zeros((1, LANE), np.float32)
    invl[0, MLA_NOPE:MLA_NOPE + MLA_ROPE] = np.concatenate([inv, inv])
    slopes = 2.0 ** (-8.0 * np.arange(1, NSA_HEADS + 1, dtype=np.float32) / NSA_HEADS)
    qconst = np.zeros((1, NSA_HEADS, LANE), np.float32)
    rest = (slopes * np.float32(LOG2E)).astype(np.float32)
    for piece in range(POS_PIECES):
        c = rest.astype(BF16).astype(np.float32)
        qconst[0, :, POS_LANE0 + 2 * piece] = c * SEL_BLOCK
        qconst[0, :, POS_LANE0 + 2 * piece + 1] = c
        rest = (rest - c).astype(np.float32)
    qconst = qconst.reshape(1, NSA_HEADS * LANE)
    gexp = np.zeros((LANE, 3 * NSA_HEADS, HEAD_DIM), np.float32)
    gexp[np.arange(3 * NSA_HEADS), np.arange(3 * NSA_HEADS), :] = 1.0
    gexp = gexp.reshape(LANE, 3 * NSA_HEADS * HEAD_DIM)

    row = lambda w: pl.BlockSpec((tm, w), lambda i: (i, 0))
    outs = [('q', 1024, BF16), ('kc', 128, BF16), ('vc', 128, BF16), ('ks', 256, BF16), ('vs', -2 * VT_ROWS, BF16),
            ('kw', 256, BF16), ('vw', -2 * VT_ROWS, BF16), ('g3', 1536, F32), ('qm', 1024, BF16),
            ('km', 1024, BF16), ('vm', -4 * VT_ROWS, BF16), ('gn', 1024, F32), ('gm', 1024, F32)]
    ospec = lambda w: row(w) if w > 0 else pl.BlockSpec((1, -w, tm), lambda i: (i, 0, 0))
    oshape = lambda w: (n, w) if w > 0 else (n // tm, -w, tm)
    res = pl.pallas_call(
        functools.partial(_inproj_kernel, tm=tm),
        grid=(n // tm,),
        in_specs=[row(D_MODEL), row(1), _const_spec((1, D_MODEL)), _const_spec((D_MODEL, W_PAD)),
                  _const_spec(w_vt.shape), _const_spec((1, MLA_Q_LORA)), _const_spec((1, MLA_KV_LORA)),
                  _const_spec(wuq.shape), _const_spec(wuqr.shape), _const_spec(wk.shape), _const_spec(wv.shape),
                  _const_spec((1, LANE)), _const_spec((1, NSA_HEADS * LANE)), _const_spec(gexp.shape)],
        out_specs=[ospec(w) for _, w, _ in outs],
        out_shape=[jax.ShapeDtypeStruct(oshape(w), dt) for _, w, dt in outs],
        compiler_params=_cparams(("parallel",)),
    )(x2, pos2, norm1_g.reshape(1, -1), w_pad, w_vt, q_norm_g.reshape(1, -1), kv_norm_g.reshape(1, -1),
      wuq, wuqr, wk, wv, jnp.asarray(invl), jnp.asarray(qconst), jnp.asarray(gexp, BF16))
    return dict(zip([o[0] for o in outs], res))


def _compress_kernel(zk_ref, zv_ref, pk_ref, pv_ref, w1k_ref, w1v_ref, w2k_ref, w2vt_ref, aug_ref,
                     kc_out, vc_out):
    half = CMP_STRIDE * NSA_GROUPS * HEAD_DIM

    def hidden(z_ref, p_ref, w1_ref):
        z = z_ref[0]
        top = _dot(z, w1_ref[0:half, :])
        bot = _dot(z, w1_ref[half:2 * half, :])
        pos = _dot(p_ref[:, 0:half], w1_ref[0:half, :]) + _dot(p_ref[:, half:2 * half], w1_ref[half:2 * half, :])
        pre = top + pltpu.roll(bot, N_CMP - 1, 0) + pos[0:1, :]
        return jax.nn.gelu(pre).astype(BF16)

    kc_out[0] = (_dot(hidden(zk_ref, pk_ref, w1k_ref), w2k_ref[...]) + aug_ref[...]).astype(BF16)
    vc_out[0] = _dot_nt(w2vt_ref[...], hidden(zv_ref, pv_ref, w1v_ref)).astype(BF16)


def _expand_w1(w1):
    w = w1.reshape(2, CMP_STRIDE, HEAD_DIM, CMP_HIDDEN)
    eye = jnp.eye(NSA_GROUPS, dtype=w1.dtype)
    w = jnp.einsum('hldc,ge->hlgdec', w, eye)
    return w.reshape(2 * CMP_STRIDE * NSA_GROUPS * HEAD_DIM, NSA_GROUPS * CMP_HIDDEN).astype(BF16)


def _expand_pos(pos):
    p = jnp.broadcast_to(pos.reshape(2, CMP_STRIDE, 1, HEAD_DIM), (2, CMP_STRIDE, NSA_GROUPS, HEAD_DIM))
    return jnp.broadcast_to(p.reshape(1, -1), (8, 2 * CMP_STRIDE * NSA_GROUPS * HEAD_DIM)).astype(BF16)


def _expand_w2(w2, dup):
    blk = jnp.concatenate([w2, w2 if dup else jnp.zeros_like(w2)], axis=-1)
    eye = jnp.eye(NSA_GROUPS, dtype=w2.dtype)
    return jnp.einsum('cd,ge->gced', blk, eye).reshape(NSA_GROUPS * CMP_HIDDEN, NSA_GROUPS * LANE).astype(BF16)


def _compress(kc_raw, vc_raw, pos_k, w1_k, w2_k, pos_v, w1_v, w2_v, batch):
    zk = kc_raw.reshape(batch, N_CMP, CMP_STRIDE * LANE)
    zv = vc_raw.reshape(batch, N_CMP, CMP_STRIDE * LANE)
    endpos = np.arange(N_CMP) * CMP_STRIDE + CMP_BLOCK - 1
    aug = np.zeros((N_CMP, NSA_GROUPS, LANE), np.float32)
    for piece in range(POS_PIECES):
        aug[:, :, POS_LANE0 + 2 * piece] = (endpos // SEL_BLOCK)[:, None]
        aug[:, :, POS_LANE0 + 2 * piece + 1] = (endpos % SEL_BLOCK)[:, None]
    aug = aug.reshape(N_CMP, NSA_GROUPS * LANE)
    zspec = pl.BlockSpec((1, N_CMP, CMP_STRIDE * LANE), lambda b: (b, 0, 0))
    ospec = pl.BlockSpec((1, N_CMP, NSA_GROUPS * LANE), lambda b: (b, 0, 0))
    args = (zk, zv, _expand_pos(pos_k), _expand_pos(pos_v), _expand_w1(w1_k), _expand_w1(w1_v),
            _expand_w2(w2_k, False), _expand_w2(w2_v, True).T, jnp.asarray(aug))
    return pl.pallas_call(
        _compress_kernel,
        grid=(batch,),
        in_specs=[zspec, zspec] + [_const_spec(a.shape) for a in args[2:]],
        out_specs=[ospec, pl.BlockSpec((1, NSA_GROUPS * LANE, N_CMP), lambda b: (b, 0, 0))],
        out_shape=[jax.ShapeDtypeStruct((batch, N_CMP, NSA_GROUPS * LANE), BF16),
                   jax.ShapeDtypeStruct((batch, NSA_GROUPS * LANE, N_CMP), BF16)],
        compiler_params=_cparams(("parallel",)),
    )(*args)


def _nsa_cmp_kernel(q_ref, kc_ref, vct_ref, gate_ref, ovt_ref, ocmp_out, selb_out, *, tq):
    qi = pl.program_id(1)
    slot = lax.broadcasted_iota(jnp.int32, (N_CMP, tq), 0)
    tslot = qi * tq + lax.broadcasted_iota(jnp.int32, (N_CMP, tq), 1)
    visible = tslot >= slot * CMP_STRIDE + (CMP_BLOCK - 1)
    sblk = lax.broadcasted_iota(jnp.int32, (N_SEL, tq), 0)
    back = ((qi * tq + lax.broadcasted_iota(jnp.int32, (N_SEL, tq), 1)) >> 6) - sblk
    causal = back >= 0
    forced = causal & ((back < N_LOCAL_FORCED) | (sblk == 0))
    low_rows = lax.broadcasted_iota(jnp.int32, (LANE, tq), 0) < HEAD_DIM

    for g in range(NSA_GROUPS):
        kc = kc_ref[0, :, g * LANE:(g + 1) * LANE]
        vct = vct_ref[0, g * LANE:(g + 1) * LANE, :]
        imp = jnp.zeros((N_SEL, tq), F32)
        outs = []
        for hh in range(NSA_HPG):
            h = g * NSA_HPG + hh
            s = jnp.where(visible, _dot_nt(kc, q_ref[:, h * LANE:(h + 1) * LANE]), NEG)
            m = jnp.max(s, axis=0, keepdims=True)
            p = jnp.where(visible, jnp.exp2(s - m), 0.0)
            p = p / jnp.maximum(jnp.sum(p, axis=0, keepdims=True), 1e-30)
            pb = p.astype(BF16)
            outs.append(_dot(vct, pb))
            imp = imp + _dot(ovt_ref[...], pb)
        for pr in range(NSA_HPG // 2):
            c0 = (g * 2 + pr) * LANE
            o = jnp.where(low_rows, outs[2 * pr], outs[2 * pr + 1]).T
            ocmp_out[:, c0:c0 + LANE] = o * gate_ref[:, c0:c0 + LANE]

        score = jnp.where(causal, imp + jnp.where(forced, FORCE_BONUS, 0.0), NEG)
        rank = jnp.zeros((N_SEL, tq), jnp.int32)
        for s2 in range(N_SEL):
            other = score[s2:s2 + 1]
            ahead = (other > score) | ((other == score) & (sblk > s2))
            rank = rank + ahead.astype(jnp.int32)
        bias = jnp.where(causal & (rank < SEL_TOPK), 0.0, NEG)
        placed = jnp.concatenate([jnp.zeros((SEL_LANE0, tq), F32), bias,
                                  jnp.zeros((LANE - SEL_LANE0 - N_SEL, tq), F32)], axis=0)
        selb_out[:, g * LANE:(g + 1) * LANE] = placed.T.astype(BF16)


def _nsa_cmp(q, kc, vct, g3, batch, *, tq=256):
    n = q.shape[0]
    nq = SEQ // tq
    cmp_start = np.arange(N_CMP) * CMP_STRIDE
    sel_start = np.arange(N_SEL) * SEL_BLOCK
    ov = np.clip(np.minimum(cmp_start[:, None] + CMP_BLOCK, sel_start[None, :] + SEL_BLOCK)
                 - np.maximum(cmp_start[:, None], sel_start[None, :]), 0, None).astype(np.float32) / CMP_BLOCK
    rows = lambda w: pl.BlockSpec((tq, w), lambda b, i: (b * nq + i, 0))
    return pl.pallas_call(
        functools.partial(_nsa_cmp_kernel, tq=tq),
        grid=(batch, nq),
        in_specs=[rows(1024), pl.BlockSpec((1, N_CMP, NSA_GROUPS * LANE), lambda b, i: (b, 0, 0)),
                  pl.BlockSpec((1, NSA_GROUPS * LANE, N_CMP), lambda b, i: (b, 0, 0)), rows(512),
                  _const_spec((N_SEL, N_CMP))],
        out_specs=[rows(512), rows(NSA_GROUPS * LANE)],
        out_shape=[jax.ShapeDtypeStruct((n, 512), F32), jax.ShapeDtypeStruct((n, NSA_GROUPS * LANE), BF16)],
        compiler_params=_cparams(("parallel", "parallel")),
    )(q, kc, vct, g3, jnp.asarray(ov.T, BF16))


def _flash_start(st, m_ref, l_ref, acc_ref, p_ref):
    m = jnp.max(st, axis=0, keepdims=True)
    m_ref[...] = m
    l_ref[...] = jnp.zeros(l_ref.shape, F32)
    acc_ref[...] = jnp.zeros(acc_ref.shape, F32)
    p_ref[...] = jnp.exp2(st - m).astype(BF16)


def _flash_step(st, vt_prev, m_ref, l_ref, acc_ref, p_ref):
    pv = _dot(vt_prev, p_ref[...])
    m_old = m_ref[...]
    m_new = jnp.maximum(m_old, jnp.max(st, axis=0, keepdims=True))
    alpha = jnp.exp2(m_old - m_new)
    l_ref[...] = (l_ref[...] + pv[LANE:LANE + 1]) * alpha
    acc_ref[...] = (acc_ref[...] + pv[0:LANE]) * alpha
    p_ref[...] = jnp.exp2(st - m_new).astype(BF16)
    m_ref[...] = m_new


def _flash_finish(vt_prev, l_ref, acc_ref, p_ref):
    pv = _dot(vt_prev, p_ref[...])
    return (acc_ref[...] + pv[0:LANE]) / (l_ref[...] + pv[LANE:LANE + 1])


def _tile_iotas(heads, t):
    key = lax.broadcasted_iota(jnp.int32, (t, heads * t), 0)
    qry = lax.broadcasted_iota(jnp.int32, (t, heads * t), 1) & (t - 1)
    return key, qry


def _nsa_attn_kernel(q_ref, selb_ref, ks_ref, vs_ref, kw_ref, vw_ref, gs_ref, gw_ref, ocmp_ref, o_out,
                     m_ref, l_ref, acc_ref, p_ref, *, t):
    qi = pl.program_id(2)
    key, qry = _tile_iotas(NSA_HPG, t)
    qs = [q_ref[:, hh * LANE:(hh + 1) * LANE] for hh in range(NSA_HPG)]
    q_win = jnp.concatenate(qs, axis=0)
    q_sel = jnp.concatenate([q + selb_ref[...] for q in qs], axis=0)
    low_rows = lax.broadcasted_iota(jnp.int32, (LANE, t), 0) < HEAD_DIM
    state = (m_ref, l_ref, acc_ref, p_ref)

    def keys(ref, j):
        return ref[pl.ds(pl.multiple_of(j * t, t), t), :]

    def finish(vt_prev):
        o = _flash_finish(vt_prev, l_ref, acc_ref, p_ref)
        return [jnp.where(low_rows, o[:, (2 * p) * t:(2 * p + 1) * t], o[:, (2 * p + 1) * t:(2 * p + 2) * t]).T
                for p in range(NSA_HPG // 2)]

    _flash_start(jnp.where(key <= qry, _dot_nt(keys(ks_ref, qi), q_sel), NEG), *state)

    def sel_body(j, prev):
        _flash_step(_dot_nt(keys(ks_ref, j), q_sel), vs_ref[prev], *state)
        return j

    prev = lax.fori_loop(0, qi, sel_body, qi)
    o_sel = finish(vs_ref[prev])

    _flash_start(jnp.where(key <= qry, _dot_nt(keys(kw_ref, qi), q_win), NEG), *state)

    @pl.when(qi >= 1)
    def _():
        _flash_step(_dot_nt(keys(kw_ref, qi - 1), q_win), vw_ref[qi], *state)

    @pl.when(qi >= 2)
    def _():
        s = _dot_nt(keys(kw_ref, qi - 2), q_win)
        _flash_step(jnp.where(key > qry, s, NEG), vw_ref[qi - 1], *state)

    o_win = finish(vw_ref[jnp.maximum(qi - 2, 0)])
    for p in range(NSA_HPG // 2):
        sl = slice(p * LANE, (p + 1) * LANE)
        o_out[:, sl] = (gs_ref[:, sl] * o_sel[p] + gw_ref[:, sl] * o_win[p] + ocmp_ref[:, sl]).astype(BF16)


def _nsa_attn(q, selb, ks, vs, kw, vw, g3, ocmp, batch, *, t=256):
    assert WINDOW == 2 * t
    n = q.shape[0]
    nq = SEQ // t
    gw = NSA_HPG * HEAD_DIM
    rows = lambda w, off=0: pl.BlockSpec((t, w), lambda b, g, i: (b * nq + i, g + off))
    seq = pl.BlockSpec((SEQ, LANE), lambda b, g, i: (b, g))
    seq_t = pl.BlockSpec((nq, VT_ROWS, t), lambda b, g, i: (b, g, 0))
    m = NSA_HPG * t
    return pl.pallas_call(
        functools.partial(_nsa_attn_kernel, t=t),
        grid=(batch, NSA_GROUPS, nq),
        in_specs=[rows(NSA_HPG * LANE), rows(LANE), seq, seq_t, seq, seq_t,
                  rows(gw, 2), rows(gw, 4), rows(gw)],
        out_specs=rows(gw),
        out_shape=jax.ShapeDtypeStruct((n, NSA_HEADS * HEAD_DIM), BF16),
        scratch_shapes=[pltpu.VMEM((1, m), F32), pltpu.VMEM((1, m), F32), pltpu.VMEM((LANE, m), F32),
                        pltpu.VMEM((t, m), BF16)],
        compiler_params=_cparams(("parallel", "parallel", "arbitrary")),
    )(q, selb, ks, vs, kw, vw, g3, g3, ocmp)


def _mla_attn_kernel(q_ref, k_ref, v_ref, o_out, m_ref, l_ref, acc_ref, p_ref, *, tq, tk):
    qi = pl.program_id(2)
    per = tq // tk
    key = lax.broadcasted_iota(jnp.int32, (tk, 2 * tq), 0)
    qry = lax.broadcasted_iota(jnp.int32, (tk, 2 * tq), 1) & (tq - 1)
    q0, q1 = q_ref[:, 0:LANE], q_ref[:, LANE:2 * LANE]

    def scores(j):
        k = k_ref[pl.ds(pl.multiple_of(j * tk, tk), tk), :]
        return jnp.concatenate([_dot_nt(k[:, 0:LANE], q0), _dot_nt(k[:, LANE:2 * LANE], q1)], axis=1)

    state = (m_ref, l_ref, acc_ref, p_ref)
    j0 = qi * per
    _flash_start(jnp.where(key <= qry, scores(j0), NEG), *state)
    for d in range(1, per):
        _flash_step(jnp.where(key + d * tk <= qry, scores(j0 + d), NEG), v_ref[j0 + d - 1], *state)

    def body(j, prev):
        _flash_step(scores(j), v_ref[prev], *state)
        return j

    prev = lax.fori_loop(0, j0, body, j0 + per - 1)
    o = _flash_finish(v_ref[prev], l_ref, acc_ref, p_ref)
    low_rows = lax.broadcasted_iota(jnp.int32, (LANE, tq), 0) < MLA_V
    o_out[...] = jnp.where(low_rows, o[:, 0:tq], o[:, tq:2 * tq]).T.astype(BF16)


def _mla_attn(qm, km, vm, batch, *, tq=512):
    n = qm.shape[0]
    nk, _, tk = vm.shape
    nq = SEQ // tq
    return pl.pallas_call(
        functools.partial(_mla_attn_kernel, tq=tq, tk=tk),
        grid=(batch, MLA_HEADS // 2, nq),
        in_specs=[pl.BlockSpec((tq, 2 * LANE), lambda b, p, i: (b * nq + i, p)),
                  pl.BlockSpec((SEQ, 2 * LANE), lambda b, p, i: (b, p)),
                  pl.BlockSpec((SEQ // tk, VT_ROWS, tk), lambda b, p, i: (b, p, 0))],
        out_specs=pl.BlockSpec((tq, LANE), lambda b, p, i: (b * nq + i, p)),
        out_shape=jax.ShapeDtypeStruct((n, MLA_HEADS * MLA_V), BF16),
        scratch_shapes=[pltpu.VMEM((1, 2 * tq), F32), pltpu.VMEM((1, 2 * tq), F32),
                        pltpu.VMEM((LANE, 2 * tq), F32), pltpu.VMEM((tk, 2 * tq), BF16)],
        compiler_params=_cparams(("parallel", "parallel", "arbitrary")),
    )(qm, km, vm)


def _mix_kernel(x_ref, on_ref, om_ref, gn_ref, gm_ref, wn_ref, wm_ref, wo_ref, g2_ref, wq_ref, k1_ref, k2_ref,
                h_out, xn_out, s_out):
    merged = gn_ref[...] * _dot(on_ref[...], wn_ref[...]) + gm_ref[...] * _dot(om_ref[...], wm_ref[...])
    h = x_ref[...] + _dot(merged.astype(BF16), wo_ref[...])
    h_out[...] = h
    xn32 = _rms(h, g2_ref[...])
    xn = xn32.astype(BF16)
    xn_out[...] = xn32.T.astype(BF16)
    half = PEER_QDIM // 2
    for hd in range(PEER_HEADS):
        q = _dot(xn, wq_ref[:, hd * PEER_QDIM:(hd + 1) * PEER_QDIM]).astype(BF16)
        s_out[2 * hd] = _dot_nt(k1_ref[...], q[:, 0:half])
        s_out[2 * hd + 1] = _dot_nt(k2_ref[...], q[:, half:PEER_QDIM])


def _mix(x2, o_nsa, o_mla, gn, gm, w_n, w_m, w_o, norm2_g, w_q, keys1, keys2, *, tm=256):
    n = x2.shape[0]
    row = lambda w: pl.BlockSpec((tm, w), lambda i: (i, 0))
    ws = [w_n.astype(BF16), w_m.astype(BF16), w_o.astype(BF16), norm2_g.reshape(1, -1), w_q.astype(BF16),
          keys1.astype(BF16), keys2.astype(BF16)]
    return pl.pallas_call(
        _mix_kernel,
        grid=(n // tm,),
        in_specs=[row(D_MODEL), row(512), row(512), row(D_MODEL), row(D_MODEL)] + [_const_spec(w.shape) for w in ws],
        out_specs=[row(D_MODEL), pl.BlockSpec((D_MODEL, tm), lambda i: (0, i)),
                   pl.BlockSpec((2 * PEER_HEADS, PEER_KEYS, tm), lambda i: (0, 0, i))],
        out_shape=[jax.ShapeDtypeStruct((n, D_MODEL), F32), jax.ShapeDtypeStruct((D_MODEL, n), BF16),
                   jax.ShapeDtypeStruct((2 * PEER_HEADS, PEER_KEYS, n), F32)],
        compiler_params=_cparams(("parallel",)),
    )(x2, o_nsa, o_mla, gn, gm, *ws)


_CAND_ROWS = 16 + 7 * 8 + 8


def _take_top(v, k, idx, exact):
    n = v.shape[0]
    rank = jnp.full(v.shape, k, jnp.int32)
    vals = []
    for r in range(k):
        m = jnp.max(v, axis=0, keepdims=True)
        hit = v == m
        if exact:
            hit = idx == jnp.min(jnp.where(hit, idx, n), axis=0, keepdims=True)
        rank = jnp.where(hit, r, rank)
        v = jnp.where(hit, -jnp.inf, v)
        vals.append(m)
    return rank, vals


def _route(s1, s2, exact):
    k = PEER_TOPK
    tl = s1.shape[1]
    kidx = lax.broadcasted_iota(jnp.int32, (PEER_KEYS, tl), 0)
    rank1, a = _take_top(s1, k, kidx, exact)
    rank2, b = _take_top(s2, k, kidx, exact)
    bcat = jnp.concatenate(b, axis=0)
    groups = [a[0] + bcat]
    for p in range(1, 8):
        groups.append(a[p] + bcat[0:8])
    groups.append(jnp.concatenate(a[8:16], axis=0) + b[0])
    cand = jnp.concatenate(groups, axis=0)
    ridx = lax.broadcasted_iota(jnp.int32, (_CAND_ROWS, tl), 0)
    qpos = jnp.where(ridx < 16, ridx, (ridx - 16) & 7)
    ppos = jnp.where(ridx < 16, 0, ((ridx - 16) >> 3) + 1)
    valid = (ridx >= 16 + 7 * 8) | ((ppos + 1) * (qpos + 1) <= k)
    crank, _ = _take_top(jnp.where(valid, cand, -jnp.inf), k, ridx, exact)
    taken = crank < k
    takenf = taken.astype(F32)
    z = jnp.sum(jnp.where(taken, jnp.exp(cand - cand[0:1]), 0.0), axis=0, keepdims=True)
    cnt = jnp.zeros((PEER_KEYS, tl), F32)
    for p in range(k):
        if p == 0:
            c = jnp.sum(takenf[0:16], axis=0, keepdims=True)
        elif p < 8:
            c = jnp.sum(takenf[16 + (p - 1) * 8:16 + p * 8], axis=0, keepdims=True)
        else:
            c = takenf[72 + (p - 8):72 + (p - 8) + 1]
        cnt = jnp.where(rank1 == p, c, cnt)
    ea = jnp.where(rank1 < k, jnp.exp(s1 - a[0]), 0.0) / z
    eb = jnp.where(rank2 < k, jnp.exp(s2 - b[0]), 0.0)
    n_taken = (jnp.sum((rank1 < k).astype(F32), axis=0, keepdims=True)
               + jnp.sum((rank2 < k).astype(F32), axis=0, keepdims=True)
               + jnp.sum(takenf, axis=0, keepdims=True))
    return (cnt, ea, rank2.astype(F32).astype(BF16), eb.astype(BF16)), jnp.max(n_taken)


def _peer_topk_kernel(s_ref, cnt_out, ea_out, r2_out, eb_out):
    s1, s2 = s_ref[0], s_ref[1]
    outs = (cnt_out, ea_out, r2_out, eb_out)
    res, n_taken = _route(s1, s2, exact=False)
    for o, r in zip(outs, res):
        o[0] = r

    @pl.when(n_taken > 3 * PEER_TOPK)
    def _():
        res, _ = _route(s1, s2, exact=True)
        for o, r in zip(outs, res):
            o[0] = r


def _peer_topk(scores, *, tl=512):
    n = scores.shape[-1]
    ospec = pl.BlockSpec((1, PEER_KEYS, tl), lambda i, h: (h, 0, i))
    shp = (PEER_HEADS, PEER_KEYS, n)
    return pl.pallas_call(
        _peer_topk_kernel,
        grid=(n // tl, PEER_HEADS),
        in_specs=[pl.BlockSpec((2, PEER_KEYS, tl), lambda i, h: (h, 0, i))],
        out_specs=[ospec] * 4,
        out_shape=[jax.ShapeDtypeStruct(shp, F32), jax.ShapeDtypeStruct(shp, F32),
                   jax.ShapeDtypeStruct(shp, BF16), jax.ShapeDtypeStruct(shp, BF16)],
        compiler_params=_cparams(("parallel", "parallel")),
    )(scores)


def _peer_dense_kernel(xnt_ref, d_ref, ut_ref, cnt_ref, ea_ref, r2_ref, eb_ref, y_out, g_ref, *, rows_per_step):
    e = pl.program_id(1)
    tl = xnt_ref.shape[1]
    chunk = 2 * PEER_KEYS

    @pl.when(e == 0)
    def _():
        y_out[...] = jnp.zeros(y_out.shape, F32)

    for c in range(rows_per_step // 2):
        hid = _dot(d_ref[c * chunk:(c + 1) * chunk, :], xnt_ref[...])
        for k2 in range(2):
            ii = 2 * c + k2
            w = None
            for hd in range(PEER_HEADS):
                cnt = jnp.broadcast_to(cnt_ref[hd, ii:ii + 1, :], (BF16_ROWS, tl)).astype(BF16)
                ea = jnp.broadcast_to(ea_ref[hd, ii:ii + 1, :], (BF16_ROWS, tl)).astype(BF16)
                wh = jnp.where(r2_ref[hd] < cnt[None], eb_ref[hd] * ea[None], jnp.zeros((), BF16))
                w = wh if w is None else w + wh
            act = _gelu_tanh(hid[k2 * PEER_KEYS:(k2 + 1) * PEER_KEYS, :]).astype(BF16)
            g_ref[ii * PEER_KEYS:(ii + 1) * PEER_KEYS, :] = act * w.reshape(PEER_KEYS, tl)
    y_out[...] += _dot(ut_ref[...], g_ref[...])


def _peer_dense(xnt, d_bf, ut_bf, cnt, ea, r2, eb, *, tl=1024, eb_rows=8):
    n = xnt.shape[1]
    ebk = eb_rows * PEER_KEYS
    groups = PEER_KEYS // BF16_ROWS
    split = lambda a: a.reshape(PEER_HEADS, groups, BF16_ROWS, n)
    full = pl.BlockSpec((PEER_HEADS, groups, BF16_ROWS, tl), lambda i, e: (0, 0, 0, i))
    part = pl.BlockSpec((PEER_HEADS, eb_rows, tl), lambda i, e: (0, e, i))
    return pl.pallas_call(
        functools.partial(_peer_dense_kernel, rows_per_step=eb_rows),
        grid=(n // tl, PEER_EXPERTS // ebk),
        in_specs=[pl.BlockSpec((D_MODEL, tl), lambda i, e: (0, i)),
                  pl.BlockSpec((ebk, D_MODEL), lambda i, e: (e, 0)),
                  pl.BlockSpec((D_MODEL, ebk), lambda i, e: (0, e)),
                  part, part, full, full],
        out_specs=pl.BlockSpec((D_MODEL, tl), lambda i, e: (0, i)),
        out_shape=jax.ShapeDtypeStruct((D_MODEL, n), F32),
        scratch_shapes=[pltpu.VMEM((ebk, tl), BF16)],
        compiler_params=_cparams(("parallel", "arbitrary")),
    )(xnt, d_bf, ut_bf, cnt, ea, split(r2), split(eb))


def _final_kernel(h_ref, yt_ref, g_ref, o_out):
    o_out[...] = _rms(h_ref[...] + yt_ref[...].T, g_ref[...])


def _final(h, yt, norm_f_g, *, tm=256):
    n = h.shape[0]
    return pl.pallas_call(
        _final_kernel,
        grid=(n // tm,),
        in_specs=[pl.BlockSpec((tm, D_MODEL), lambda i: (i, 0)), pl.BlockSpec((D_MODEL, tm), lambda i: (0, i)),
                  _const_spec((1, D_MODEL))],
        out_specs=pl.BlockSpec((tm, D_MODEL), lambda i: (i, 0)),
        out_shape=jax.ShapeDtypeStruct((n, D_MODEL), F32),
        compiler_params=_cparams(("parallel",)),
    )(h, yt, norm_f_g.reshape(1, -1))


def _layer(h2, pos2, batch, p):
    a = _inproj(h2, pos2, p['norm1_g'], p['w_in'], p['mla_q_norm_g'], p['mla_kv_norm_g'], p['mla_w_uq'],
                p['mla_w_ukv'])
    kc, vc = _compress(a['kc'], a['vc'], p['nsa_cmp_pos_k'], p['nsa_cmp_w1_k'], p['nsa_cmp_w2_k'],
                       p['nsa_cmp_pos_v'], p['nsa_cmp_w1_v'], p['nsa_cmp_w2_v'], batch)
    ocmp, selb = _nsa_cmp(a['q'], kc, vc, a['g3'], batch)
    o_nsa = _nsa_attn(a['q'], selb, a['ks'], a['vs'], a['kw'], a['vw'], a['g3'], ocmp, batch)
    o_mla = _mla_attn(a['qm'], a['km'], a['vm'], batch)
    h_mid, xn, scores = _mix(h2, o_nsa, o_mla, a['gn'], a['gm'], p['w_branch_nsa'], p['w_branch_mla'], p['w_out'],
                             p['norm2_g'], p['peer_w_query'], p['peer_sub_keys_1'], p['peer_sub_keys_2'])
    cnt, ea, r2, eb = _peer_topk(scores)
    yt = _peer_dense(xn, p['peer_expert_down'].astype(BF16), p['peer_expert_up'].T.astype(BF16), cnt, ea, r2, eb)
    return h_mid, yt


def kernel(x, positions, norm1_g, w_in, nsa_cmp_pos_k, nsa_cmp_w1_k, nsa_cmp_w2_k, nsa_cmp_pos_v, nsa_cmp_w1_v,
           nsa_cmp_w2_v, mla_q_norm_g, mla_kv_norm_g, mla_w_uq, mla_w_ukv, w_branch_nsa, w_branch_mla, w_out,
           norm2_g, peer_w_query, peer_sub_keys_1, peer_sub_keys_2, peer_expert_down, peer_expert_up, norm_f_g):
    batch, seq, d = x.shape
    assert seq == SEQ and d == D_MODEL
    stacked = dict(norm1_g=norm1_g, w_in=w_in, nsa_cmp_pos_k=nsa_cmp_pos_k, nsa_cmp_w1_k=nsa_cmp_w1_k,
                   nsa_cmp_w2_k=nsa_cmp_w2_k, nsa_cmp_pos_v=nsa_cmp_pos_v, nsa_cmp_w1_v=nsa_cmp_w1_v,
                   nsa_cmp_w2_v=nsa_cmp_w2_v, mla_q_norm_g=mla_q_norm_g, mla_kv_norm_g=mla_kv_norm_g,
                   mla_w_uq=mla_w_uq, mla_w_ukv=mla_w_ukv, w_branch_nsa=w_branch_nsa, w_branch_mla=w_branch_mla,
                   w_out=w_out, norm2_g=norm2_g, peer_w_query=peer_w_query, peer_sub_keys_1=peer_sub_keys_1,
                   peer_sub_keys_2=peer_sub_keys_2, peer_expert_down=peer_expert_down,
                   peer_expert_up=peer_expert_up)
    assert w_in.shape[0] == 1, "single-layer block"
    p = {k: v[0] for k, v in stacked.items()}
    h_mid, yt = _layer(x.reshape(batch * seq, d), positions.reshape(batch * seq, 1), batch, p)
    return _final(h_mid, yt, norm_f_g).reshape(batch, seq, d)
```

```python
import functools
import math

import numpy as np
import jax
import jax.numpy as jnp
from jax import lax
from jax.experimental import pallas as pl
from jax.experimental.pallas import tpu as pltpu

F32 = jnp.float32
BF16 = jnp.bfloat16

D_MODEL = 1024
SEQ = 2048
EPS = 1e-6
NEG = -1e30
FORCE_BONUS = 1e4

NSA_HEADS = 8
NSA_GROUPS = 2
NSA_HPG = 4
HEAD_DIM = 64
CMP_BLOCK = 32
CMP_STRIDE = 16
CMP_HIDDEN = 128
N_CMP = SEQ // CMP_STRIDE
SEL_BLOCK = 64
N_SEL = SEQ // SEL_BLOCK
SEL_TOPK = 16
N_LOCAL_FORCED = 2
WINDOW = 512

MLA_HEADS = 8
MLA_Q_LORA = 256
MLA_KV_LORA = 128
MLA_NOPE = 64
MLA_ROPE = 32
MLA_V = 64
ROPE_THETA = 10000.0

PEER_HEADS = 8
PEER_KEYS = 128
PEER_EXPERTS = PEER_KEYS * PEER_KEYS
PEER_QDIM = 256
PEER_TOPK = 16

LANE = 128
BF16_ROWS = 16
VT_ROWS = LANE + BF16_ROWS
SEL_LANE0 = 64
POS_LANE0 = 96
POS_PIECES = 3
LOG2E = math.log2(math.e)

VMEM_LIMIT = 56 * 1024 * 1024

IN_SPLITS = (512, 128, 128, 128, 128, 128, 128, 24, 256, 128, 32, 1024, 1024)


def _cparams(sem, flags=None):
    return pltpu.CompilerParams(dimension_semantics=sem, vmem_limit_bytes=VMEM_LIMIT, flags=flags)


def _dot(a, b):
    return jnp.dot(a, b, preferred_element_type=F32)


def _dot_nt(a, b):
    return lax.dot_general(a, b, (((1,), (1,)), ((), ())), preferred_element_type=F32)


def _rms(x, g):
    return x * lax.rsqrt(jnp.mean(x * x, axis=-1, keepdims=True) + EPS) * g


_GELU_K1 = -2.0 * math.sqrt(2.0 / math.pi) * math.log2(math.e)
_GELU_K2 = _GELU_K1 * 0.044715


def _gelu_tanh(x):
    return x / (1.0 + jnp.exp2(x * (_GELU_K1 + _GELU_K2 * (x * x))))


def _const_spec(shape):
    nd = len(shape)
    return pl.BlockSpec(shape, lambda *_: (0,) * nd)


_SEG = dict(q=1024, kc=128, vc=128, ks=256, kw=256, g3=128,
            cq=256, ckv=128, kr=128, krr=128, gn=1024, gm=1024)
_SEG_OFF = {}
_o = 0
for _k, _w in _SEG.items():
    _SEG_OFF[_k] = (_o, _o + _w)
    _o += _w
W_PAD = _o


def _pad_heads(w, n_heads, width):
    k = w.shape[0]
    return jnp.pad(w.reshape(k, n_heads, width), ((0, 0), (0, 0), (0, LANE - width))).reshape(k, n_heads * LANE)


def _dup_heads(w, n_heads, width):
    k = w.shape[0]
    w3 = w.reshape(k, n_heads, width)
    return jnp.concatenate([w3, w3], axis=-1).reshape(k, n_heads * 2 * width)


def _rope_rot_cols(w_pe):
    half = MLA_ROPE // 2
    return jnp.concatenate([-w_pe[..., half:], w_pe[..., :half]], axis=-1)


def _build_w_in(w_in):
    offs = np.cumsum((0,) + IN_SPLITS)
    seg = [w_in[:, offs[i]:offs[i + 1]] for i in range(len(IN_SPLITS))]
    wq, wkc, wvc, wks, wvs, wkw, wvw, wg, wcq, wckv, wkr, wgn, wgm = seg
    k = w_in.shape[0]
    g3 = jnp.pad(wg, ((0, 0), (0, LANE - 3 * NSA_HEADS)))
    kr = jnp.pad(wkr, ((0, 0), (SEL_LANE0, LANE - SEL_LANE0 - MLA_ROPE)))
    krr = jnp.pad(_rope_rot_cols(wkr), ((0, 0), (SEL_LANE0, LANE - SEL_LANE0 - MLA_ROPE)))
    cols = [_pad_heads(wq * (HEAD_DIM ** -0.5 * LOG2E), NSA_HEADS, HEAD_DIM), wkc, wvc,
            _pad_heads(wks, NSA_GROUPS, HEAD_DIM), _pad_heads(wkw, NSA_GROUPS, HEAD_DIM),
            g3, wcq, wckv, kr, krr, wgn, wgm]
    w_vt = jnp.concatenate([_dup_heads(wvs, NSA_GROUPS, HEAD_DIM), _dup_heads(wvw, NSA_GROUPS, HEAD_DIM)], axis=1).T
    return jnp.concatenate(cols, axis=1).astype(BF16), w_vt.astype(BF16)


def _inproj_kernel(x_ref, pos_ref, g1_ref, w_ref, wvt_ref, qg_ref, kvg_ref, wuq_ref, wuqr_ref, wk_ref, wv_ref,
                   invl_ref, qconst_ref, gexp_ref,
                   q_out, kc_out, vc_out, ks_out, vs_out, kw_out, vw_out, g3_out,
                   qm_out, km_out, vm_out, gn_out, gm_out, *, tm):
    xn = _rms(x_ref[...], g1_ref[...]).astype(BF16)

    def proj(name, c0=0, c1=None):
        a, b = _SEG_OFF[name]
        c1 = b - a if c1 is None else c1
        return _dot(xn, w_ref[:, a + c0:a + c1])

    for h in range(NSA_HEADS):
        sl = slice(h * LANE, (h + 1) * LANE)
        q_out[:, sl] = (proj('q', h * LANE, (h + 1) * LANE) + qconst_ref[:, sl]).astype(BF16)
    kc_out[...] = proj('kc').astype(BF16)
    vc_out[...] = proj('vc').astype(BF16)

    row = lax.broadcasted_iota(jnp.int32, (tm, LANE), 0)
    lane = lax.broadcasted_iota(jnp.int32, (tm, LANE), 1)
    t = (pl.program_id(0) * tm + row) & (SEQ - 1)
    blk = t >> 6
    in_pos = (lane >= POS_LANE0) & (lane < POS_LANE0 + 2 * POS_PIECES)
    pos_aug = jnp.where(in_pos, jnp.where((lane & 1) == 0, blk, t & 63), 0).astype(F32)
    sel_aug = pos_aug + jnp.where(lane == SEL_LANE0 + blk, 1.0, 0.0)
    for g in range(NSA_GROUPS):
        sl = slice(g * LANE, (g + 1) * LANE)
        ks_out[:, sl] = (proj('ks', g * LANE, (g + 1) * LANE) + sel_aug).astype(BF16)
        kw_out[:, sl] = (proj('kw', g * LANE, (g + 1) * LANE) + pos_aug).astype(BF16)
    def store_vt(out, vals):
        for g in range(vals.shape[0] // LANE):
            out[0, g * VT_ROWS:g * VT_ROWS + LANE, :] = vals[g * LANE:(g + 1) * LANE].astype(BF16)
            out[0, g * VT_ROWS + LANE:(g + 1) * VT_ROWS, :] = jnp.ones((BF16_ROWS, tm), BF16)

    store_vt(vs_out, _dot_nt(wvt_ref[0:2 * LANE, :], xn))
    store_vt(vw_out, _dot_nt(wvt_ref[2 * LANE:4 * LANE, :], xn))
    gate = jax.nn.sigmoid(proj('g3'))
    g_hi = gate.astype(BF16)
    g_lo = (gate - g_hi.astype(F32)).astype(BF16)
    for c in range(3):
        ex = gexp_ref[:, c * 512:(c + 1) * 512]
        g3_out[:, c * 512:(c + 1) * 512] = _dot(g_hi, ex) + _dot(g_lo, ex)
    for c in range(2):
        gn_out[:, c * 512:(c + 1) * 512] = jax.nn.sigmoid(proj('gn', c * 512, (c + 1) * 512))
        gm_out[:, c * 512:(c + 1) * 512] = jax.nn.sigmoid(proj('gm', c * 512, (c + 1) * 512))

    ang = pos_ref[...].astype(F32) * invl_ref[...]
    cos, sin = jnp.cos(ang), jnp.sin(ang)
    cqn = _rms(proj('cq'), qg_ref[...]).astype(BF16)
    ckvn = _rms(proj('ckv'), kvg_ref[...]).astype(BF16)
    kpe = proj('kr') * cos + proj('krr') * sin
    qscale = (MLA_NOPE + MLA_ROPE) ** -0.5 * LOG2E
    for h in range(MLA_HEADS):
        sl = slice(h * LANE, (h + 1) * LANE)
        qh = _dot(cqn, wuq_ref[:, sl]) * cos + _dot(cqn, wuqr_ref[:, sl]) * sin
        qm_out[:, sl] = (qh * qscale).astype(BF16)
        km_out[:, sl] = (_dot(ckvn, wk_ref[:, sl]) + kpe).astype(BF16)
    store_vt(vm_out, _dot_nt(wv_ref[...], ckvn))


def _inproj(x2, pos2, norm1_g, w_in, q_norm_g, kv_norm_g, w_uq, w_ukv, *, tm=256):
    n = x2.shape[0]
    w_pad, w_vt = _build_w_in(w_in)
    uq3 = w_uq.reshape(MLA_Q_LORA, MLA_HEADS, MLA_NOPE + MLA_ROPE)
    wuq = jnp.pad(uq3, ((0, 0), (0, 0), (0, LANE - MLA_NOPE - MLA_ROPE))).reshape(MLA_Q_LORA, -1).astype(BF16)
    uq_rot = jnp.concatenate([jnp.zeros_like(uq3[..., :MLA_NOPE]), _rope_rot_cols(uq3[..., MLA_NOPE:])], axis=-1)
    wuqr = jnp.pad(uq_rot, ((0, 0), (0, 0), (0, LANE - MLA_NOPE - MLA_ROPE))).reshape(MLA_Q_LORA, -1).astype(BF16)
    ukv3 = w_ukv.reshape(MLA_KV_LORA, MLA_HEADS, MLA_NOPE + MLA_V)
    wk = jnp.pad(ukv3[..., :MLA_NOPE], ((0, 0), (0, 0), (0, LANE - MLA_NOPE))).reshape(MLA_KV_LORA, -1).astype(BF16)
    wv = ukv3[..., MLA_NOPE:].reshape(MLA_KV_LORA, -1).T.astype(BF16)

    inv = ROPE_THETA ** (-np.arange(0, MLA_ROPE, 2, dtype=np.float32) / MLA_ROPE)
    invl = np.zeros((1, LANE), np.float32)
    invl[0, MLA_NOPE:MLA_NOPE + MLA_ROPE] = np.concatenate([inv, inv])
    slopes = 2.0 ** (-8.0 * np.arange(1, NSA_HEADS + 1, dtype=np.float32) / NSA_HEADS)
    qconst = np.zeros((1, NSA_HEADS, LANE), np.float32)
    rest = (slopes * np.float32(LOG2E)).astype(np.float32)
    for piece in range(POS_PIECES):
        c = rest.astype(BF16).astype(np.float32)
        qconst[0, :, POS_LANE0 + 2 * piece] = c * SEL_BLOCK
        qconst[0, :, POS_LANE0 + 2 * piece + 1] = c
        rest = (rest - c).astype(np.float32)
    qconst = qconst.reshape(1, NSA_HEADS * LANE)
    gexp = np.zeros((LANE, 3 * NSA_HEADS, HEAD_DIM), np.float32)
    gexp[np.arange(3 * NSA_HEADS), np.arange(3 * NSA_HEADS), :] = 1.0
    gexp = gexp.reshape(LANE, 3 * NSA_HEADS * HEAD_DIM)

    row = lambda w: pl.BlockSpec((tm, w), lambda i: (i, 0))
    outs = [('q', 1024, BF16), ('kc', 128, BF16), ('vc', 128, BF16), ('ks', 256, BF16), ('vs', -2 * VT_ROWS, BF16),
            ('kw', 256, BF16), ('vw', -2 * VT_ROWS, BF16), ('g3', 1536, F32), ('qm', 1024, BF16),
            ('km', 1024, BF16), ('vm', -4 * VT_ROWS, BF16), ('gn', 1024, F32), ('gm', 1024, F32)]
    ospec = lambda w: row(w) if w > 0 else pl.BlockSpec((1, -w, tm), lambda i: (i, 0, 0))
    oshape = lambda w: (n, w) if w > 0 else (n // tm, -w, tm)
    res = pl.pallas_call(
        functools.partial(_inproj_kernel, tm=tm),
        grid=(n // tm,),
        in_specs=[row(D_MODEL), row(1), _const_spec((1, D_MODEL)), _const_spec((D_MODEL, W_PAD)),
                  _const_spec(w_vt.shape), _const_spec((1, MLA_Q_LORA)), _const_spec((1, MLA_KV_LORA)),
                  _const_spec(wuq.shape), _const_spec(wuqr.shape), _const_spec(wk.shape), _const_spec(wv.shape),
                  _const_spec((1, LANE)), _const_spec((1, NSA_HEADS * LANE)), _const_spec(gexp.shape)],
        out_specs=[ospec(w) for _, w, _ in outs],
        out_shape=[jax.ShapeDtypeStruct(oshape(w), dt) for _, w, dt in outs],
        compiler_params=_cparams(("parallel",)),
    )(x2, pos2, norm1_g.reshape(1, -1), w_pad, w_vt, q_norm_g.reshape(1, -1), kv_norm_g.reshape(1, -1),
      wuq, wuqr, wk, wv, jnp.asarray(invl), jnp.asarray(qconst), jnp.asarray(gexp, BF16))
    return dict(zip([o[0] for o in outs], res))


def _compress_kernel(zk_ref, zv_ref, pk_ref, pv_ref, w1k_ref, w1v_ref, w2k_ref, w2vt_ref, aug_ref,
                     kc_out, vc_out):
    half = CMP_STRIDE * NSA_GROUPS * HEAD_DIM

    def hidden(z_ref, p_ref, w1_ref):
        z = z_ref[0]
        top = _dot(z, w1_ref[0:half, :])
        bot = _dot(z, w1_ref[half:2 * half, :])
        pos = _dot(p_ref[:, 0:half], w1_ref[0:half, :]) + _dot(p_ref[:, half:2 * half], w1_ref[half:2 * half, :])
        pre = top + pltpu.roll(bot, N_CMP - 1, 0) + pos[0:1, :]
        return jax.nn.gelu(pre).astype(BF16)

    kc_out[0] = (_dot(hidden(zk_ref, pk_ref, w1k_ref), w2k_ref[...]) + aug_ref[...]).astype(BF16)
    vc_out[0] = _dot_nt(w2vt_ref[...], hidden(zv_ref, pv_ref, w1v_ref)).astype(BF16)


def _expand_w1(w1):
    w = w1.reshape(2, CMP_STRIDE, HEAD_DIM, CMP_HIDDEN)
    eye = jnp.eye(NSA_GROUPS, dtype=w1.dtype)
    w = jnp.einsum('hldc,ge->hlgdec', w, eye)
    return w.reshape(2 * CMP_STRIDE * NSA_GROUPS * HEAD_DIM, NSA_GROUPS * CMP_HIDDEN).astype(BF16)


def _expand_pos(pos):
    p = jnp.broadcast_to(pos.reshape(2, CMP_STRIDE, 1, HEAD_DIM), (2, CMP_STRIDE, NSA_GROUPS, HEAD_DIM))
    return jnp.broadcast_to(p.reshape(1, -1), (8, 2 * CMP_STRIDE * NSA_GROUPS * HEAD_DIM)).astype(BF16)


def _expand_w2(w2, dup):
    blk = jnp.concatenate([w2, w2 if dup else jnp.zeros_like(w2)], axis=-1)
    eye = jnp.eye(NSA_GROUPS, dtype=w2.dtype)
    return jnp.einsum('cd,ge->gced', blk, eye).reshape(NSA_GROUPS * CMP_HIDDEN, NSA_GROUPS * LANE).astype(BF16)


def _compress(kc_raw, vc_raw, pos_k, w1_k, w2_k, pos_v, w1_v, w2_v, batch):
    zk = kc_raw.reshape(batch, N_CMP, CMP_STRIDE * LANE)
    zv = vc_raw.reshape(batch, N_CMP, CMP_STRIDE * LANE)
    endpos = np.arange(N_CMP) * CMP_STRIDE + CMP_BLOCK - 1
    aug = np.zeros((N_CMP, NSA_GROUPS, LANE), np.float32)
    for piece in range(POS_PIECES):
        aug[:, :, POS_LANE0 + 2 * piece] = (endpos // SEL_BLOCK)[:, None]
        aug[:, :, POS_LANE0 + 2 * piece + 1] = (endpos % SEL_BLOCK)[:, None]
    aug = aug.reshape(N_CMP, NSA_GROUPS * LANE)
    zspec = pl.BlockSpec((1, N_CMP, CMP_STRIDE * LANE), lambda b: (b, 0, 0))
    ospec = pl.BlockSpec((1, N_CMP, NSA_GROUPS * LANE), lambda b: (b, 0, 0))
    args = (zk, zv, _expand_pos(pos_k), _expand_pos(pos_v), _expand_w1(w1_k), _expand_w1(w1_v),
            _expand_w2(w2_k, False), _expand_w2(w2_v, True).T, jnp.asarray(aug))
    return pl.pallas_call(
        _compress_kernel,
        grid=(batch,),
        in_specs=[zspec, zspec] + [_const_spec(a.shape) for a in args[2:]],
        out_specs=[ospec, pl.BlockSpec((1, NSA_GROUPS * LANE, N_CMP), lambda b: (b, 0, 0))],
        out_shape=[jax.ShapeDtypeStruct((batch, N_CMP, NSA_GROUPS * LANE), BF16),
                   jax.ShapeDtypeStruct((batch, NSA_GROUPS * LANE, N_CMP), BF16)],
        compiler_params=_cparams(("parallel",)),
    )(*args)


def _nsa_cmp_kernel(q_ref, kc_ref, vct_ref, gate_ref, ovt_ref, ocmp_out, selb_out, *, tq):
    qi = pl.program_id(1)
    slot = lax.broadcasted_iota(jnp.int32, (N_CMP, tq), 0)
    tslot = qi * tq + lax.broadcasted_iota(jnp.int32, (N_CMP, tq), 1)
    visible = tslot >= slot * CMP_STRIDE + (CMP_BLOCK - 1)
    sblk = lax.broadcasted_iota(jnp.int32, (N_SEL, tq), 0)
    back = ((qi * tq + lax.broadcasted_iota(jnp.int32, (N_SEL, tq), 1)) >> 6) - sblk
    causal = back >= 0
    forced = causal & ((back < N_LOCAL_FORCED) | (sblk == 0))
    low_rows = lax.broadcasted_iota(jnp.int32, (LANE, tq), 0) < HEAD_DIM

    for g in range(NSA_GROUPS):
        kc = kc_ref[0, :, g * LANE:(g + 1) * LANE]
        vct = vct_ref[0, g * LANE:(g + 1) * LANE, :]
        imp = jnp.zeros((N_SEL, tq), F32)
        outs = []
        for hh in range(NSA_HPG):
            h = g * NSA_HPG + hh
            s = jnp.where(visible, _dot_nt(kc, q_ref[:, h * LANE:(h + 1) * LANE]), NEG)
            m = jnp.max(s, axis=0, keepdims=True)
            p = jnp.where(visible, jnp.exp2(s - m), 0.0)
            p = p / jnp.maximum(jnp.sum(p, axis=0, keepdims=True), 1e-30)
            pb = p.astype(BF16)
            outs.append(_dot(vct, pb))
            imp = imp + _dot(ovt_ref[...], pb)
        for pr in range(NSA_HPG // 2):
            c0 = (g * 2 + pr) * LANE
            o = jnp.where(low_rows, outs[2 * pr], outs[2 * pr + 1]).T
            ocmp_out[:, c0:c0 + LANE] = o * gate_ref[:, c0:c0 + LANE]

        score = jnp.where(causal, imp + jnp.where(forced, FORCE_BONUS, 0.0), NEG)
        rank = jnp.zeros((N_SEL, tq), jnp.int32)
        for s2 in range(N_SEL):
            other = score[s2:s2 + 1]
            ahead = (other > score) | ((other == score) & (sblk > s2))
            rank = rank + ahead.astype(jnp.int32)
        bias = jnp.where(causal & (rank < SEL_TOPK), 0.0, NEG)
        placed = jnp.concatenate([jnp.zeros((SEL_LANE0, tq), F32), bias,
                                  jnp.zeros((LANE - SEL_LANE0 - N_SEL, tq), F32)], axis=0)
        selb_out[:, g * LANE:(g + 1) * LANE] = placed.T.astype(BF16)


def _nsa_cmp(q, kc, vct, g3, batch, *, tq=256):
    n = q.shape[0]
    nq = SEQ // tq
    cmp_start = np.arange(N_CMP) * CMP_STRIDE
    sel_start = np.arange(N_SEL) * SEL_BLOCK
    ov = np.clip(np.minimum(cmp_start[:, None] + CMP_BLOCK, sel_start[None, :] + SEL_BLOCK)
                 - np.maximum(cmp_start[:, None], sel_start[None, :]), 0, None).astype(np.float32) / CMP_BLOCK
    rows = lambda w: pl.BlockSpec((tq, w), lambda b, i: (b * nq + i, 0))
    return pl.pallas_call(
        functools.partial(_nsa_cmp_kernel, tq=tq),
        grid=(batch, nq),
        in_specs=[rows(1024), pl.BlockSpec((1, N_CMP, NSA_GROUPS * LANE), lambda b, i: (b, 0, 0)),
                  pl.BlockSpec((1, NSA_GROUPS * LANE, N_CMP), lambda b, i: (b, 0, 0)), rows(512),
                  _const_spec((N_SEL, N_CMP))],
        out_specs=[rows(512), rows(NSA_GROUPS * LANE)],
        out_shape=[jax.ShapeDtypeStruct((n, 512), F32), jax.ShapeDtypeStruct((n, NSA_GROUPS * LANE), BF16)],
        compiler_params=_cparams(("parallel", "parallel")),
    )(q, kc, vct, g3, jnp.asarray(ov.T, BF16))


def _flash_start(st, m_ref, l_ref, acc_ref, p_ref):
    m = jnp.max(st, axis=0, keepdims=True)
    m_ref[...] = m
    l_ref[...] = jnp.zeros(l_ref.shape, F32)
    acc_ref[...] = jnp.zeros(acc_ref.shape, F32)
    p_ref[...] = jnp.exp2(st - m).astype(BF16)


def _flash_step(st, vt_prev, m_ref, l_ref, acc_ref, p_ref):
    pv = _dot(vt_prev, p_ref[...])
    m_old = m_ref[...]
    m_new = jnp.maximum(m_old, jnp.max(st, axis=0, keepdims=True))
    alpha = jnp.exp2(m_old - m_new)
    l_ref[...] = (l_ref[...] + pv[LANE:LANE + 1]) * alpha
    acc_ref[...] = (acc_ref[...] + pv[0:LANE]) * alpha
    p_ref[...] = jnp.exp2(st - m_new).astype(BF16)
    m_ref[...] = m_new


def _flash_finish(vt_prev, l_ref, acc_ref, p_ref):
    pv = _dot(vt_prev, p_ref[...])
    return (acc_ref[...] + pv[0:LANE]) / (l_ref[...] + pv[LANE:LANE + 1])


def _tile_iotas(heads, t):
    key = lax.broadcasted_iota(jnp.int32, (t, heads * t), 0)
    qry = lax.broadcasted_iota(jnp.int32, (t, heads * t), 1) & (t - 1)
    return key, qry


def _nsa_attn_kernel(q_ref, selb_ref, ks_ref, vs_ref, kw_ref, vw_ref, gs_ref, gw_ref, ocmp_ref, o_out,
                     m_ref, l_ref, acc_ref, p_ref, *, t):
    qi = pl.program_id(2)
    key, qry = _tile_iotas(NSA_HPG, t)
    qs = [q_ref[:, hh * LANE:(hh + 1) * LANE] for hh in range(NSA_HPG)]
    q_win = jnp.concatenate(qs, axis=0)
    q_sel = jnp.concatenate([q + selb_ref[...] for q in qs], axis=0)
    low_rows = lax.broadcasted_iota(jnp.int32, (LANE, t), 0) < HEAD_DIM
    state = (m_ref, l_ref, acc_ref, p_ref)

    def keys(ref, j):
        return ref[pl.ds(pl.multiple_of(j * t, t), t), :]

    def finish(vt_prev):
        o = _flash_finish(vt_prev, l_ref, acc_ref, p_ref)
        return [jnp.where(low_rows, o[:, (2 * p) * t:(2 * p + 1) * t], o[:, (2 * p + 1) * t:(2 * p + 2) * t]).T
                for p in range(NSA_HPG // 2)]

    _flash_start(jnp.where(key <= qry, _dot_nt(keys(ks_ref, qi), q_sel), NEG), *state)

    def sel_body(j, prev):
        _flash_step(_dot_nt(keys(ks_ref, j), q_sel), vs_ref[prev], *state)
        return j

    prev = lax.fori_loop(0, qi, sel_body, qi)
    o_sel = finish(vs_ref[prev])

    _flash_start(jnp.where(key <= qry, _dot_nt(keys(kw_ref, qi), q_win), NEG), *state)

    @pl.when(qi >= 1)
    def _():
        _flash_step(_dot_nt(keys(kw_ref, qi - 1), q_win), vw_ref[qi], *state)

    @pl.when(qi >= 2)
    def _():
        s = _dot_nt(keys(kw_ref, qi - 2), q_win)
        _flash_step(jnp.where(key > qry, s, NEG), vw_ref[qi - 1], *state)

    o_win = finish(vw_ref[jnp.maximum(qi - 2, 0)])
    for p in range(NSA_HPG // 2):
        sl = slice(p * LANE, (p + 1) * LANE)
        o_out[:, sl] = (gs_ref[:, sl] * o_sel[p] + gw_ref[:, sl] * o_win[p] + ocmp_ref[:, sl]).astype(BF16)


def _nsa_attn(q, selb, ks, vs, kw, vw, g3, ocmp, batch, *, t=256):
    assert WINDOW == 2 * t
    n = q.shape[0]
    nq = SEQ // t
    gw = NSA_HPG * HEAD_DIM
    rows = lambda w, off=0: pl.BlockSpec((t, w), lambda b, g, i: (b * nq + i, g + off))
    seq = pl.BlockSpec((SEQ, LANE), lambda b, g, i: (b, g))
    seq_t = pl.BlockSpec((nq, VT_ROWS, t), lambda b, g, i: (b, g, 0))
    m = NSA_HPG * t
    return pl.pallas_call(
        functools.partial(_nsa_attn_kernel, t=t),
        grid=(batch, NSA_GROUPS, nq),
        in_specs=[rows(NSA_HPG * LANE), rows(LANE), seq, seq_t, seq, seq_t,
                  rows(gw, 2), rows(gw, 4), rows(gw)],
        out_specs=rows(gw),
        out_shape=jax.ShapeDtypeStruct((n, NSA_HEADS * HEAD_DIM), BF16),
        scratch_shapes=[pltpu.VMEM((1, m), F32), pltpu.VMEM((1, m), F32), pltpu.VMEM((LANE, m), F32),
                        pltpu.VMEM((t, m), BF16)],
        compiler_params=_cparams(("parallel", "parallel", "arbitrary")),
    )(q, selb, ks, vs, kw, vw, g3, g3, ocmp)


def _mla_attn_kernel(q_ref, k_ref, v_ref, o_out, m_ref, l_ref, acc_ref, p_ref, *, tq, tk):
    qi = pl.program_id(2)
    per = tq // tk
    key = lax.broadcasted_iota(jnp.int32, (tk, 2 * tq), 0)
    qry = lax.broadcasted_iota(jnp.int32, (tk, 2 * tq), 1) & (tq - 1)
    q0, q1 = q_ref[:, 0:LANE], q_ref[:, LANE:2 * LANE]

    def scores(j):
        k = k_ref[pl.ds(pl.multiple_of(j * tk, tk), tk), :]
        return jnp.concatenate([_dot_nt(k[:, 0:LANE], q0), _dot_nt(k[:, LANE:2 * LANE], q1)], axis=1)

    state = (m_ref, l_ref, acc_ref, p_ref)
    j0 = qi * per
    _flash_start(jnp.where(key <= qry, scores(j0), NEG), *state)
    for d in range(1, per):
        _flash_step(jnp.where(key + d * tk <= qry, scores(j0 + d), NEG), v_ref[j0 + d - 1], *state)

    def body(j, prev):
        _flash_step(scores(j), v_ref[prev], *state)
        return j

    prev = lax.fori_loop(0, j0, body, j0 + per - 1)
    o = _flash_finish(v_ref[prev], l_ref, acc_ref, p_ref)
    low_rows = lax.broadcasted_iota(jnp.int32, (LANE, tq), 0) < MLA_V
    o_out[...] = jnp.where(low_rows, o[:, 0:tq], o[:, tq:2 * tq]).T.astype(BF16)


def _mla_attn(qm, km, vm, batch, *, tq=512):
    n = qm.shape[0]
    nk, _, tk = vm.shape
    nq = SEQ // tq
    return pl.pallas_call(
        functools.partial(_mla_attn_kernel, tq=tq, tk=tk),
        grid=(batch, MLA_HEADS // 2, nq),
        in_specs=[pl.BlockSpec((tq, 2 * LANE), lambda b, p, i: (b * nq + i, p)),
                  pl.BlockSpec((SEQ, 2 * LANE), lambda b, p, i: (b, p)),
                  pl.BlockSpec((SEQ // tk, VT_ROWS, tk), lambda b, p, i: (b, p, 0))],
        out_specs=pl.BlockSpec((tq, LANE), lambda b, p, i: (b * nq + i, p)),
        out_shape=jax.ShapeDtypeStruct((n, MLA_HEADS * MLA_V), BF16),
        scratch_shapes=[pltpu.VMEM((1, 2 * tq), F32), pltpu.VMEM((1, 2 * tq), F32),
                        pltpu.VMEM((LANE, 2 * tq), F32), pltpu.VMEM((tk, 2 * tq), BF16)],
        compiler_params=_cparams(("parallel", "parallel", "arbitrary")),
    )(qm, km, vm)


def _mix_kernel(x_ref, on_ref, om_ref, gn_ref, gm_ref, wn_ref, wm_ref, wo_ref, g2_ref, wq_ref, k1_ref, k2_ref,
                h_out, xn_out, s_out, *, sub):
    subs = [slice(r0, r0 + sub) for r0 in range(0, x_ref.shape[0], sub)]
    merged = [gn_ref[s, :] * _dot(on_ref[s, :], wn_ref[...]) + gm_ref[s, :] * _dot(om_ref[s, :], wm_ref[...])
              for s in subs]
    hs = [x_ref[s, :] + _dot(mg.astype(BF16), wo_ref[...]) for s, mg in zip(subs, merged)]
    xns = []
    for s, h in zip(subs, hs):
        h_out[s, :] = h
        xn32 = _rms(h, g2_ref[...])
        xn_out[:, s] = xn32.T.astype(BF16)
        xns.append(xn32.astype(BF16))
    half = PEER_QDIM // 2
    for hd in range(PEER_HEADS):
        qs = [_dot(xn, wq_ref[:, hd * PEER_QDIM:(hd + 1) * PEER_QDIM]).astype(BF16) for xn in xns]
        for s, q in zip(subs, qs):
            s_out[2 * hd, :, s] = _dot_nt(k1_ref[...], q[:, 0:half])
            s_out[2 * hd + 1, :, s] = _dot_nt(k2_ref[...], q[:, half:PEER_QDIM])


def _mix(x2, o_nsa, o_mla, gn, gm, w_n, w_m, w_o, norm2_g, w_q, keys1, keys2, *, tm=512):
    n = x2.shape[0]
    row = lambda w: pl.BlockSpec((tm, w), lambda i: (i, 0))
    ws = [w_n.astype(BF16), w_m.astype(BF16), w_o.astype(BF16), norm2_g.reshape(1, -1), w_q.astype(BF16),
          keys1.astype(BF16), keys2.astype(BF16)]
    return pl.pallas_call(
        functools.partial(_mix_kernel, sub=256),
        grid=(n // tm,),
        in_specs=[row(D_MODEL), row(512), row(512), row(D_MODEL), row(D_MODEL)] + [_const_spec(w.shape) for w in ws],
        out_specs=[row(D_MODEL), pl.BlockSpec((D_MODEL, tm), lambda i: (0, i)),
                   pl.BlockSpec((2 * PEER_HEADS, PEER_KEYS, tm), lambda i: (0, 0, i))],
        out_shape=[jax.ShapeDtypeStruct((n, D_MODEL), F32), jax.ShapeDtypeStruct((D_MODEL, n), BF16),
                   jax.ShapeDtypeStruct((2 * PEER_HEADS, PEER_KEYS, n), F32)],
        compiler_params=_cparams(("parallel",)),
    )(x2, o_nsa, o_mla, gn, gm, *ws)


_CAND_ROWS = 16 + 7 * 8 + 8


def _take_top(v, k, idx, exact):
    n = v.shape[0]
    rank = jnp.full(v.shape, k, jnp.int32)
    vals = []
    for r in range(k):
        m = jnp.max(v, axis=0, keepdims=True)
        hit = v == m
        if exact:
            hit = idx == jnp.min(jnp.where(hit, idx, n), axis=0, keepdims=True)
        rank = jnp.where(hit, r, rank)
        v = jnp.where(hit, -jnp.inf, v)
        vals.append(m)
    return rank, vals


def _route(s1, s2, exact):
    k = PEER_TOPK
    tl = s1.shape[1]
    kidx = lax.broadcasted_iota(jnp.int32, (PEER_KEYS, tl), 0)
    rank1, a = _take_top(s1, k, kidx, exact)
    rank2, b = _take_top(s2, k, kidx, exact)
    bcat = jnp.concatenate(b, axis=0)
    groups = [a[0] + bcat]
    for p in range(1, 8):
        groups.append(a[p] + bcat[0:8])
    groups.append(jnp.concatenate(a[8:16], axis=0) + b[0])
    cand = jnp.concatenate(groups, axis=0)
    ridx = lax.broadcasted_iota(jnp.int32, (_CAND_ROWS, tl), 0)
    qpos = jnp.where(ridx < 16, ridx, (ridx - 16) & 7)
    ppos = jnp.where(ridx < 16, 0, ((ridx - 16) >> 3) + 1)
    valid = (ridx >= 16 + 7 * 8) | ((ppos + 1) * (qpos + 1) <= k)
    crank, _ = _take_top(jnp.where(valid, cand, -jnp.inf), k, ridx, exact)
    taken = crank < k
    takenf = taken.astype(F32)
    z = jnp.sum(jnp.where(taken, jnp.exp(cand - cand[0:1]), 0.0), axis=0, keepdims=True)
    cnt = jnp.zeros((PEER_KEYS, tl), F32)
    for p in range(k):
        if p == 0:
            c = jnp.sum(takenf[0:16], axis=0, keepdims=True)
        elif p < 8:
            c = jnp.sum(takenf[16 + (p - 1) * 8:16 + p * 8], axis=0, keepdims=True)
        else:
            c = takenf[72 + (p - 8):72 + (p - 8) + 1]
        cnt = jnp.where(rank1 == p, c, cnt)
    ea = jnp.where(rank1 < k, jnp.exp(s1 - a[0]), 0.0) / z
    eb = jnp.where(rank2 < k, jnp.exp(s2 - b[0]), 0.0)
    n_taken = (jnp.sum((rank1 < k).astype(F32), axis=0, keepdims=True)
               + jnp.sum((rank2 < k).astype(F32), axis=0, keepdims=True)
               + jnp.sum(takenf, axis=0, keepdims=True))
    return (cnt, ea, rank2.astype(F32).astype(BF16), eb.astype(BF16)), jnp.max(n_taken)


def _peer_topk_kernel(s_ref, cnt_out, ea_out, r2_out, eb_out):
    s1, s2 = s_ref[0], s_ref[1]
    outs = (cnt_out, ea_out, r2_out, eb_out)
    res, n_taken = _route(s1, s2, exact=False)
    for o, r in zip(outs, res):
        o[0] = r

    @pl.when(n_taken > 3 * PEER_TOPK)
    def _():
        res, _ = _route(s1, s2, exact=True)
        for o, r in zip(outs, res):
            o[0] = r


def _peer_topk(scores, *, tl=512):
    n = scores.shape[-1]
    ospec = pl.BlockSpec((1, PEER_KEYS, tl), lambda i, h: (h, 0, i))
    shp = (PEER_HEADS, PEER_KEYS, n)
    return pl.pallas_call(
        _peer_topk_kernel,
        grid=(n // tl, PEER_HEADS),
        in_specs=[pl.BlockSpec((2, PEER_KEYS, tl), lambda i, h: (h, 0, i))],
        out_specs=[ospec] * 4,
        out_shape=[jax.ShapeDtypeStruct(shp, F32), jax.ShapeDtypeStruct(shp, F32),
                   jax.ShapeDtypeStruct(shp, BF16), jax.ShapeDtypeStruct(shp, BF16)],
        compiler_params=_cparams(("parallel", "parallel")),
    )(scores)


def _peer_dense_kernel(xnt_ref, d_ref, ut_ref, cnt_ref, ea_ref, r2_ref, eb_ref, y_out, g_ref, *, rows_per_step):
    e = pl.program_id(1)
    tl = xnt_ref.shape[1]
    chunk = 2 * PEER_KEYS

    @pl.when(e == 0)
    def _():
        y_out[...] = jnp.zeros(y_out.shape, F32)

    for c in range(rows_per_step // 2):
        hid = _dot(d_ref[c * chunk:(c + 1) * chunk, :], xnt_ref[...])
        for k2 in range(2):
            ii = 2 * c + k2
            w = None
            for hd in range(PEER_HEADS):
                cnt = jnp.broadcast_to(cnt_ref[hd, ii:ii + 1, :], (BF16_ROWS, tl)).astype(BF16)
                ea = jnp.broadcast_to(ea_ref[hd, ii:ii + 1, :], (BF16_ROWS, tl)).astype(BF16)
                wh = jnp.where(r2_ref[hd] < cnt[None], eb_ref[hd] * ea[None], jnp.zeros((), BF16))
                w = wh if w is None else w + wh
            act = _gelu_tanh(hid[k2 * PEER_KEYS:(k2 + 1) * PEER_KEYS, :]).astype(BF16)
            g_ref[ii * PEER_KEYS:(ii + 1) * PEER_KEYS, :] = act * w.reshape(PEER_KEYS, tl)
    y_out[...] += _dot(ut_ref[...], g_ref[...])


def _peer_dense(xnt, d_bf, ut_bf, cnt, ea, r2, eb, *, tl=1024, eb_rows=8):
    n = xnt.shape[1]
    ebk = eb_rows * PEER_KEYS
    groups = PEER_KEYS // BF16_ROWS
    split = lambda a: a.reshape(PEER_HEADS, groups, BF16_ROWS, n)
    full = pl.BlockSpec((PEER_HEADS, groups, BF16_ROWS, tl), lambda i, e: (0, 0, 0, i))
    part = pl.BlockSpec((PEER_HEADS, eb_rows, tl), lambda i, e: (0, e, i))
    return pl.pallas_call(
        functools.partial(_peer_dense_kernel, rows_per_step=eb_rows),
        grid=(n // tl, PEER_EXPERTS // ebk),
        in_specs=[pl.BlockSpec((D_MODEL, tl), lambda i, e: (0, i)),
                  pl.BlockSpec((ebk, D_MODEL), lambda i, e: (e, 0)),
                  pl.BlockSpec((D_MODEL, ebk), lambda i, e: (0, e)),
                  part, part, full, full],
        out_specs=pl.BlockSpec((D_MODEL, tl), lambda i, e: (0, i)),
        out_shape=jax.ShapeDtypeStruct((D_MODEL, n), F32),
        scratch_shapes=[pltpu.VMEM((ebk, tl), BF16)],
        compiler_params=_cparams(("parallel", "arbitrary")),
    )(xnt, d_bf, ut_bf, cnt, ea, split(r2), split(eb))


def _final_kernel(h_ref, yt_ref, g_ref, o_out):
    o_out[...] = _rms(h_ref[...] + yt_ref[...].T, g_ref[...])


def _final(h, yt, norm_f_g, *, tm=256):
    n = h.shape[0]
    return pl.pallas_call(
        _final_kernel,
        grid=(n // tm,),
        in_specs=[pl.BlockSpec((tm, D_MODEL), lambda i: (i, 0)), pl.BlockSpec((D_MODEL, tm), lambda i: (0, i)),
                  _const_spec((1, D_MODEL))],
        out_specs=pl.BlockSpec((tm, D_MODEL), lambda i: (i, 0)),
        out_shape=jax.ShapeDtypeStruct((n, D_MODEL), F32),
        compiler_params=_cparams(("parallel",)),
    )(h, yt, norm_f_g.reshape(1, -1))


def _layer(h2, pos2, batch, p):
    a = _inproj(h2, pos2, p['norm1_g'], p['w_in'], p['mla_q_norm_g'], p['mla_kv_norm_g'], p['mla_w_uq'],
                p['mla_w_ukv'])
    kc, vc = _compress(a['kc'], a['vc'], p['nsa_cmp_pos_k'], p['nsa_cmp_w1_k'], p['nsa_cmp_w2_k'],
                       p['nsa_cmp_pos_v'], p['nsa_cmp_w1_v'], p['nsa_cmp_w2_v'], batch)
    ocmp, selb = _nsa_cmp(a['q'], kc, vc, a['g3'], batch)
    o_nsa = _nsa_attn(a['q'], selb, a['ks'], a['vs'], a['kw'], a['vw'], a['g3'], ocmp, batch)
    o_mla = _mla_attn(a['qm'], a['km'], a['vm'], batch)
    h_mid, xn, scores = _mix(h2, o_nsa, o_mla, a['gn'], a['gm'], p['w_branch_nsa'], p['w_branch_mla'], p['w_out'],
                             p['norm2_g'], p['peer_w_query'], p['peer_sub_keys_1'], p['peer_sub_keys_2'])
    cnt, ea, r2, eb = _peer_topk(scores)
    yt = _peer_dense(xn, p['peer_expert_down'].astype(BF16), p['peer_expert_up'].T.astype(BF16), cnt, ea, r2, eb)
    return h_mid, yt


def kernel(x, positions, norm1_g, w_in, nsa_cmp_pos_k, nsa_cmp_w1_k, nsa_cmp_w2_k, nsa_cmp_pos_v, nsa_cmp_w1_v,
           nsa_cmp_w2_v, mla_q_norm_g, mla_kv_norm_g, mla_w_uq, mla_w_ukv, w_branch_nsa, w_branch_mla, w_out,
           norm2_g, peer_w_query, peer_sub_keys_1, peer_sub_keys_2, peer_expert_down, peer_expert_up, norm_f_g):
    batch, seq, d = x.shape
    assert seq == SEQ and d == D_MODEL
    stacked = dict(norm1_g=norm1_g, w_in=w_in, nsa_cmp_pos_k=nsa_cmp_pos_k, nsa_cmp_w1_k=nsa_cmp_w1_k,
                   nsa_cmp_w2_k=nsa_cmp_w2_k, nsa_cmp_pos_v=nsa_cmp_pos_v, nsa_cmp_w1_v=nsa_cmp_w1_v,
                   nsa_cmp_w2_v=nsa_cmp_w2_v, mla_q_norm_g=mla_q_norm_g, mla_kv_norm_g=mla_kv_norm_g,
                   mla_w_uq=mla_w_uq, mla_w_ukv=mla_w_ukv, w_branch_nsa=w_branch_nsa, w_branch_mla=w_branch_mla,
                   w_out=w_out, norm2_g=norm2_g, peer_w_query=peer_w_query, peer_sub_keys_1=peer_sub_keys_1,
                   peer_sub_keys_2=peer_sub_keys_2, peer_expert_down=peer_expert_down,
                   peer_expert_up=peer_expert_up)
    assert w_in.shape[0] == 1, "single-layer block"
    p = {k: v[0] for k, v in stacked.items()}
    h_mid, yt = _layer(x.reshape(batch * seq, d), positions.reshape(batch * seq, 1), batch, p)
    return _final(h_mid, yt, norm_f_g).reshape(batch, seq, d)
```

```python
import functools
import math

import numpy as np
import jax
import jax.numpy as jnp
from jax import lax
from jax.experimental import pallas as pl
from jax.experimental.pallas import tpu as pltpu

F32 = jnp.float32
BF16 = jnp.bfloat16

D_MODEL = 1024
SEQ = 2048
EPS = 1e-6
NEG = -1e30
FORCE_BONUS = 1e4

NSA_HEADS = 8
NSA_GROUPS = 2
NSA_HPG = 4
HEAD_DIM = 64
CMP_BLOCK = 32
CMP_STRIDE = 16
CMP_HIDDEN = 128
N_CMP = SEQ // CMP_STRIDE
SEL_BLOCK = 64
N_SEL = SEQ // SEL_BLOCK
SEL_TOPK = 16
N_LOCAL_FORCED = 2
WINDOW = 512

MLA_HEADS = 8
MLA_Q_LORA = 256
MLA_KV_LORA = 128
MLA_NOPE = 64
MLA_ROPE = 32
MLA_V = 64
ROPE_THETA = 10000.0

PEER_HEADS = 8
PEER_KEYS = 128
PEER_EXPERTS = PEER_KEYS * PEER_KEYS
PEER_QDIM = 256
PEER_TOPK = 16

LANE = 128
BF16_ROWS = 16
VT_ROWS = LANE + BF16_ROWS
SEL_LANE0 = 64
POS_LANE0 = 96
POS_PIECES = 3
LOG2E = math.log2(math.e)

VMEM_LIMIT = 56 * 1024 * 1024

IN_SPLITS = (512, 128, 128, 128, 128, 128, 128, 24, 256, 128, 32, 1024, 1024)


def _cparams(sem, flags=None):
    return pltpu.CompilerParams(dimension_semantics=sem, vmem_limit_bytes=VMEM_LIMIT, flags=flags)


def _dot(a, b):
    return jnp.dot(a, b, preferred_element_type=F32)


def _dot_nt(a, b):
    return lax.dot_general(a, b, (((1,), (1,)), ((), ())), preferred_element_type=F32)


def _rms(x, g):
    return x * lax.rsqrt(jnp.mean(x * x, axis=-1, keepdims=True) + EPS) * g


_GELU_K1 = -2.0 * math.sqrt(2.0 / math.pi) * math.log2(math.e)
_GELU_K2 = _GELU_K1 * 0.044715


def _gelu_tanh(x):
    return x / (1.0 + jnp.exp2(x * (_GELU_K1 + _GELU_K2 * (x * x))))


def _const_spec(shape):
    nd = len(shape)
    return pl.BlockSpec(shape, lambda *_: (0,) * nd)


_SEG = dict(q=1024, kc=128, vc=128, ks=256, kw=256, g3=128,
            cq=256, ckv=128, kr=128, krr=128, gn=1024, gm=1024)
_SEG_OFF = {}
_o = 0
for _k, _w in _SEG.items():
    _SEG_OFF[_k] = (_o, _o + _w)
    _o += _w
W_PAD = _o


def _pad_heads(w, n_heads, width):
    k = w.shape[0]
    return jnp.pad(w.reshape(k, n_heads, width), ((0, 0), (0, 0), (0, LANE - width))).reshape(k, n_heads * LANE)


def _dup_heads(w, n_heads, width):
    k = w.shape[0]
    w3 = w.reshape(k, n_heads, width)
    return jnp.concatenate([w3, w3], axis=-1).reshape(k, n_heads * 2 * width)


def _rope_rot_cols(w_pe):
    half = MLA_ROPE // 2
    return jnp.concatenate([-w_pe[..., half:], w_pe[..., :half]], axis=-1)


def _build_w_in(w_in):
    offs = np.cumsum((0,) + IN_SPLITS)
    seg = [w_in[:, offs[i]:offs[i + 1]] for i in range(len(IN_SPLITS))]
    wq, wkc, wvc, wks, wvs, wkw, wvw, wg, wcq, wckv, wkr, wgn, wgm = seg
    k = w_in.shape[0]
    g3 = jnp.pad(wg, ((0, 0), (0, LANE - 3 * NSA_HEADS)))
    kr = jnp.pad(wkr, ((0, 0), (SEL_LANE0, LANE - SEL_LANE0 - MLA_ROPE)))
    krr = jnp.pad(_rope_rot_cols(wkr), ((0, 0), (SEL_LANE0, LANE - SEL_LANE0 - MLA_ROPE)))
    cols = [_pad_heads(wq * (HEAD_DIM ** -0.5 * LOG2E), NSA_HEADS, HEAD_DIM), wkc, wvc,
            _pad_heads(wks, NSA_GROUPS, HEAD_DIM), _pad_heads(wkw, NSA_GROUPS, HEAD_DIM),
            g3, wcq, wckv, kr, krr, wgn, wgm]
    w_vt = jnp.concatenate([_dup_heads(wvs, NSA_GROUPS, HEAD_DIM), _dup_heads(wvw, NSA_GROUPS, HEAD_DIM)], axis=1).T
    return jnp.concatenate(cols, axis=1).astype(BF16), w_vt.astype(BF16)


def _inproj_kernel(x_ref, pos_ref, g1_ref, w_ref, wvt_ref, qg_ref, kvg_ref, wuq_ref, wuqr_ref, wk_ref, wv_ref,
                   invl_ref, qconst_ref, gexp_ref,
                   q_out, kc_out, vc_out, ks_out, vs_out, kw_out, vw_out, g3_out,
                   qm_out, km_out, vm_out, gn_out, gm_out, *, tm, sub):
    subs = [slice(r0, r0 + sub) for r0 in range(0, tm, sub)]
    xns = [_rms(x_ref[s, :], g1_ref[...]).astype(BF16) for s in subs]

    def proj(name, c0=0, c1=None):
        a, b = _SEG_OFF[name]
        c1 = b - a if c1 is None else c1
        return [_dot(xn, w_ref[:, a + c0:a + c1]) for xn in xns]

    for h in range(NSA_HEADS):
        sl = slice(h * LANE, (h + 1) * LANE)
        for s, v in zip(subs, proj('q', h * LANE, (h + 1) * LANE)):
            q_out[s, sl] = (v + qconst_ref[:, sl]).astype(BF16)
    for s, v in zip(subs, proj('kc')):
        kc_out[s, :] = v.astype(BF16)
    for s, v in zip(subs, proj('vc')):
        vc_out[s, :] = v.astype(BF16)

    row = lax.broadcasted_iota(jnp.int32, (sub, LANE), 0)
    lane = lax.broadcasted_iota(jnp.int32, (sub, LANE), 1)
    in_pos = (lane >= POS_LANE0) & (lane < POS_LANE0 + 2 * POS_PIECES)
    pos_augs, sel_augs = [], []
    for s in subs:
        t = (pl.program_id(0) * tm + s.start + row) & (SEQ - 1)
        blk = t >> 6
        pos_aug = jnp.where(in_pos, jnp.where((lane & 1) == 0, blk, t & 63), 0).astype(F32)
        pos_augs.append(pos_aug)
        sel_augs.append(pos_aug + jnp.where(lane == SEL_LANE0 + blk, 1.0, 0.0))
    for g in range(NSA_GROUPS):
        sl = slice(g * LANE, (g + 1) * LANE)
        for s, v, aug in zip(subs, proj('ks', g * LANE, (g + 1) * LANE), sel_augs):
            ks_out[s, sl] = (v + aug).astype(BF16)
        for s, v, aug in zip(subs, proj('kw', g * LANE, (g + 1) * LANE), pos_augs):
            kw_out[s, sl] = (v + aug).astype(BF16)

    def store_vt(out, vals_list):
        for i, vals in enumerate(vals_list):
            for g in range(vals.shape[0] // LANE):
                out[i, g * VT_ROWS:g * VT_ROWS + LANE, :] = vals[g * LANE:(g + 1) * LANE].astype(BF16)
                out[i, g * VT_ROWS + LANE:(g + 1) * VT_ROWS, :] = jnp.ones((BF16_ROWS, sub), BF16)

    store_vt(vs_out, [_dot_nt(wvt_ref[0:2 * LANE, :], xn) for xn in xns])
    store_vt(vw_out, [_dot_nt(wvt_ref[2 * LANE:4 * LANE, :], xn) for xn in xns])
    gates = [jax.nn.sigmoid(v) for v in proj('g3')]
    g_his = [gate.astype(BF16) for gate in gates]
    g_los = [(gate - g_hi.astype(F32)).astype(BF16) for gate, g_hi in zip(gates, g_his)]
    for c in range(3):
        ex = gexp_ref[:, c * 512:(c + 1) * 512]
        for s, g_hi, g_lo in zip(subs, g_his, g_los):
            g3_out[s, c * 512:(c + 1) * 512] = _dot(g_hi, ex) + _dot(g_lo, ex)
    for c in range(2):
        for s, v in zip(subs, proj('gn', c * 512, (c + 1) * 512)):
            gn_out[s, c * 512:(c + 1) * 512] = jax.nn.sigmoid(v)
        for s, v in zip(subs, proj('gm', c * 512, (c + 1) * 512)):
            gm_out[s, c * 512:(c + 1) * 512] = jax.nn.sigmoid(v)

    angs = [pos_ref[s, :].astype(F32) * invl_ref[...] for s in subs]
    coss, sins = [jnp.cos(a) for a in angs], [jnp.sin(a) for a in angs]
    cqns = [_rms(v, qg_ref[...]).astype(BF16) for v in proj('cq')]
    ckvns = [_rms(v, kvg_ref[...]).astype(BF16) for v in proj('ckv')]
    kpes = [kr * c + krr * sn for kr, krr, c, sn in zip(proj('kr'), proj('krr'), coss, sins)]
    qscale = (MLA_NOPE + MLA_ROPE) ** -0.5 * LOG2E
    for h in range(MLA_HEADS):
        sl = slice(h * LANE, (h + 1) * LANE)
        for s, cqn, ckvn, c, sn, kpe in zip(subs, cqns, ckvns, coss, sins, kpes):
            qh = _dot(cqn, wuq_ref[:, sl]) * c + _dot(cqn, wuqr_ref[:, sl]) * sn
            qm_out[s, sl] = (qh * qscale).astype(BF16)
            km_out[s, sl] = (_dot(ckvn, wk_ref[:, sl]) + kpe).astype(BF16)
    store_vt(vm_out, [_dot_nt(wv_ref[...], ckvn) for ckvn in ckvns])


def _inproj(x2, pos2, norm1_g, w_in, q_norm_g, kv_norm_g, w_uq, w_ukv, *, tm=512, sub=256):
    n = x2.shape[0]
    w_pad, w_vt = _build_w_in(w_in)
    uq3 = w_uq.reshape(MLA_Q_LORA, MLA_HEADS, MLA_NOPE + MLA_ROPE)
    wuq = jnp.pad(uq3, ((0, 0), (0, 0), (0, LANE - MLA_NOPE - MLA_ROPE))).reshape(MLA_Q_LORA, -1).astype(BF16)
    uq_rot = jnp.concatenate([jnp.zeros_like(uq3[..., :MLA_NOPE]), _rope_rot_cols(uq3[..., MLA_NOPE:])], axis=-1)
    wuqr = jnp.pad(uq_rot, ((0, 0), (0, 0), (0, LANE - MLA_NOPE - MLA_ROPE))).reshape(MLA_Q_LORA, -1).astype(BF16)
    ukv3 = w_ukv.reshape(MLA_KV_LORA, MLA_HEADS, MLA_NOPE + MLA_V)
    wk = jnp.pad(ukv3[..., :MLA_NOPE], ((0, 0), (0, 0), (0, LANE - MLA_NOPE))).reshape(MLA_KV_LORA, -1).astype(BF16)
    wv = ukv3[..., MLA_NOPE:].reshape(MLA_KV_LORA, -1).T.astype(BF16)

    inv = ROPE_THETA ** (-np.arange(0, MLA_ROPE, 2, dtype=np.float32) / MLA_ROPE)
    invl = np.zeros((1, LANE), np.float32)
    invl[0, MLA_NOPE:MLA_NOPE + MLA_ROPE] = np.concatenate([inv, inv])
    slopes = 2.0 ** (-8.0 * np.arange(1, NSA_HEADS + 1, dtype=np.float32) / NSA_HEADS)
    qconst = np.zeros((1, NSA_HEADS, LANE), np.float32)
    rest = (slopes * np.float32(LOG2E)).astype(np.float32)
    for piece in range(POS_PIECES):
        c = rest.astype(BF16).astype(np.float32)
        qconst[0, :, POS_LANE0 + 2 * piece] = c * SEL_BLOCK
        qconst[0, :, POS_LANE0 + 2 * piece + 1] = c
        rest = (rest - c).astype(np.float32)
    qconst = qconst.reshape(1, NSA_HEADS * LANE)
    gexp = np.zeros((LANE, 3 * NSA_HEADS, HEAD_DIM), np.float32)
    gexp[np.arange(3 * NSA_HEADS), np.arange(3 * NSA_HEADS), :] = 1.0
    gexp = gexp.reshape(LANE, 3 * NSA_HEADS * HEAD_DIM)

    row = lambda w: pl.BlockSpec((tm, w), lambda i: (i, 0))
    outs = [('q', 1024, BF16), ('kc', 128, BF16), ('vc', 128, BF16), ('ks', 256, BF16), ('vs', -2 * VT_ROWS, BF16),
            ('kw', 256, BF16), ('vw', -2 * VT_ROWS, BF16), ('g3', 1536, F32), ('qm', 1024, BF16),
            ('km', 1024, BF16), ('vm', -4 * VT_ROWS, BF16), ('gn', 1024, F32), ('gm', 1024, F32)]
    ospec = lambda w: row(w) if w > 0 else pl.BlockSpec((tm // sub, -w, sub), lambda i: (i, 0, 0))
    oshape = lambda w: (n, w) if w > 0 else (n // sub, -w, sub)
    res = pl.pallas_call(
        functools.partial(_inproj_kernel, tm=tm, sub=sub),
        grid=(n // tm,),
        in_specs=[row(D_MODEL), row(1), _const_spec((1, D_MODEL)), _const_spec((D_MODEL, W_PAD)),
                  _const_spec(w_vt.shape), _const_spec((1, MLA_Q_LORA)), _const_spec((1, MLA_KV_LORA)),
                  _const_spec(wuq.shape), _const_spec(wuqr.shape), _const_spec(wk.shape), _const_spec(wv.shape),
                  _const_spec((1, LANE)), _const_spec((1, NSA_HEADS * LANE)), _const_spec(gexp.shape)],
        out_specs=[ospec(w) for _, w, _ in outs],
        out_shape=[jax.ShapeDtypeStruct(oshape(w), dt) for _, w, dt in outs],
        compiler_params=_cparams(("parallel",)),
    )(x2, pos2, norm1_g.reshape(1, -1), w_pad, w_vt, q_norm_g.reshape(1, -1), kv_norm_g.reshape(1, -1),
      wuq, wuqr, wk, wv, jnp.asarray(invl), jnp.asarray(qconst), jnp.asarray(gexp, BF16))
    return dict(zip([o[0] for o in outs], res))


def _compress_kernel(zk_ref, zv_ref, pk_ref, pv_ref, w1k_ref, w1v_ref, w2k_ref, w2vt_ref, aug_ref,
                     kc_out, vc_out):
    half = CMP_STRIDE * NSA_GROUPS * HEAD_DIM

    def hidden(z_ref, p_ref, w1_ref):
        z = z_ref[0]
        top = _dot(z, w1_ref[0:half, :])
        bot = _dot(z, w1_ref[half:2 * half, :])
        pos = _dot(p_ref[:, 0:half], w1_ref[0:half, :]) + _dot(p_ref[:, half:2 * half], w1_ref[half:2 * half, :])
        pre = top + pltpu.roll(bot, N_CMP - 1, 0) + pos[0:1, :]
        return jax.nn.gelu(pre).astype(BF16)

    kc_out[0] = (_dot(hidden(zk_ref, pk_ref, w1k_ref), w2k_ref[...]) + aug_ref[...]).astype(BF16)
    vc_out[0] = _dot_nt(w2vt_ref[...], hidden(zv_ref, pv_ref, w1v_ref)).astype(BF16)


def _expand_w1(w1):
    w = w1.reshape(2, CMP_STRIDE, HEAD_DIM, CMP_HIDDEN)
    eye = jnp.eye(NSA_GROUPS, dtype=w1.dtype)
    w = jnp.einsum('hldc,ge->hlgdec', w, eye)
    return w.reshape(2 * CMP_STRIDE * NSA_GROUPS * HEAD_DIM, NSA_GROUPS * CMP_HIDDEN).astype(BF16)


def _expand_pos(pos):
    p = jnp.broadcast_to(pos.reshape(2, CMP_STRIDE, 1, HEAD_DIM), (2, CMP_STRIDE, NSA_GROUPS, HEAD_DIM))
    return jnp.broadcast_to(p.reshape(1, -1), (8, 2 * CMP_STRIDE * NSA_GROUPS * HEAD_DIM)).astype(BF16)


def _expand_w2(w2, dup):
    blk = jnp.concatenate([w2, w2 if dup else jnp.zeros_like(w2)], axis=-1)
    eye = jnp.eye(NSA_GROUPS, dtype=w2.dtype)
    return jnp.einsum('cd,ge->gced', blk, eye).reshape(NSA_GROUPS * CMP_HIDDEN, NSA_GROUPS * LANE).astype(BF16)


def _compress(kc_raw, vc_raw, pos_k, w1_k, w2_k, pos_v, w1_v, w2_v, batch):
    zk = kc_raw.reshape(batch, N_CMP, CMP_STRIDE * LANE)
    zv = vc_raw.reshape(batch, N_CMP, CMP_STRIDE * LANE)
    endpos = np.arange(N_CMP) * CMP_STRIDE + CMP_BLOCK - 1
    aug = np.zeros((N_CMP, NSA_GROUPS, LANE), np.float32)
    for piece in range(POS_PIECES):
        aug[:, :, POS_LANE0 + 2 * piece] = (endpos // SEL_BLOCK)[:, None]
        aug[:, :, POS_LANE0 + 2 * piece + 1] = (endpos % SEL_BLOCK)[:, None]
    aug = aug.reshape(N_CMP, NSA_GROUPS * LANE)
    zspec = pl.BlockSpec((1, N_CMP, CMP_STRIDE * LANE), lambda b: (b, 0, 0))
    ospec = pl.BlockSpec((1, N_CMP, NSA_GROUPS * LANE), lambda b: (b, 0, 0))
    args = (zk, zv, _expand_pos(pos_k), _expand_pos(pos_v), _expand_w1(w1_k), _expand_w1(w1_v),
            _expand_w2(w2_k, False), _expand_w2(w2_v, True).T, jnp.asarray(aug))
    return pl.pallas_call(
        _compress_kernel,
        grid=(batch,),
        in_specs=[zspec, zspec] + [_const_spec(a.shape) for a in args[2:]],
        out_specs=[ospec, pl.BlockSpec((1, NSA_GROUPS * LANE, N_CMP), lambda b: (b, 0, 0))],
        out_shape=[jax.ShapeDtypeStruct((batch, N_CMP, NSA_GROUPS * LANE), BF16),
                   jax.ShapeDtypeStruct((batch, NSA_GROUPS * LANE, N_CMP), BF16)],
        compiler_params=_cparams(("parallel",)),
    )(*args)


def _nsa_cmp_kernel(q_ref, kc_ref, vct_ref, gate_ref, ovt_ref, ocmp_out, selb_out, *, tq):
    qi = pl.program_id(1)
    slot = lax.broadcasted_iota(jnp.int32, (N_CMP, tq), 0)
    tslot = qi * tq + lax.broadcasted_iota(jnp.int32, (N_CMP, tq), 1)
    visible = tslot >= slot * CMP_STRIDE + (CMP_BLOCK - 1)
    sblk = lax.broadcasted_iota(jnp.int32, (N_SEL, tq), 0)
    back = ((qi * tq + lax.broadcasted_iota(jnp.int32, (N_SEL, tq), 1)) >> 6) - sblk
    causal = back >= 0
    forced = causal & ((back < N_LOCAL_FORCED) | (sblk == 0))
    low_rows = lax.broadcasted_iota(jnp.int32, (LANE, tq), 0) < HEAD_DIM

    for g in range(NSA_GROUPS):
        kc = kc_ref[0, :, g * LANE:(g + 1) * LANE]
        vct = vct_ref[0, g * LANE:(g + 1) * LANE, :]
        imp = jnp.zeros((N_SEL, tq), F32)
        outs = []
        for hh in range(NSA_HPG):
            h = g * NSA_HPG + hh
            s = jnp.where(visible, _dot_nt(kc, q_ref[:, h * LANE:(h + 1) * LANE]), NEG)
            m = jnp.max(s, axis=0, keepdims=True)
            p = jnp.where(visible, jnp.exp2(s - m), 0.0)
            p = p / jnp.maximum(jnp.sum(p, axis=0, keepdims=True), 1e-30)
            pb = p.astype(BF16)
            outs.append(_dot(vct, pb))
            imp = imp + _dot(ovt_ref[...], pb)
        for pr in range(NSA_HPG // 2):
            c0 = (g * 2 + pr) * LANE
            o = jnp.where(low_rows, outs[2 * pr], outs[2 * pr + 1]).T
            ocmp_out[:, c0:c0 + LANE] = o * gate_ref[:, c0:c0 + LANE]

        score = jnp.where(causal, imp + jnp.where(forced, FORCE_BONUS, 0.0), NEG)
        rank = jnp.zeros((N_SEL, tq), jnp.int32)
        for s2 in range(N_SEL):
            other = score[s2:s2 + 1]
            ahead = (other > score) | ((other == score) & (sblk > s2))
            rank = rank + ahead.astype(jnp.int32)
        bias = jnp.where(causal & (rank < SEL_TOPK), 0.0, NEG)
        placed = jnp.concatenate([jnp.zeros((SEL_LANE0, tq), F32), bias,
                                  jnp.zeros((LANE - SEL_LANE0 - N_SEL, tq), F32)], axis=0)
        selb_out[:, g * LANE:(g + 1) * LANE] = placed.T.astype(BF16)


def _nsa_cmp(q, kc, vct, g3, batch, *, tq=256):
    n = q.shape[0]
    nq = SEQ // tq
    cmp_start = np.arange(N_CMP) * CMP_STRIDE
    sel_start = np.arange(N_SEL) * SEL_BLOCK
    ov = np.clip(np.minimum(cmp_start[:, None] + CMP_BLOCK, sel_start[None, :] + SEL_BLOCK)
                 - np.maximum(cmp_start[:, None], sel_start[None, :]), 0, None).astype(np.float32) / CMP_BLOCK
    rows = lambda w: pl.BlockSpec((tq, w), lambda b, i: (b * nq + i, 0))
    return pl.pallas_call(
        functools.partial(_nsa_cmp_kernel, tq=tq),
        grid=(batch, nq),
        in_specs=[rows(1024), pl.BlockSpec((1, N_CMP, NSA_GROUPS * LANE), lambda b, i: (b, 0, 0)),
                  pl.BlockSpec((1, NSA_GROUPS * LANE, N_CMP), lambda b, i: (b, 0, 0)), rows(512),
                  _const_spec((N_SEL, N_CMP))],
        out_specs=[rows(512), rows(NSA_GROUPS * LANE)],
        out_shape=[jax.ShapeDtypeStruct((n, 512), F32), jax.ShapeDtypeStruct((n, NSA_GROUPS * LANE), BF16)],
        compiler_params=_cparams(("parallel", "parallel")),
    )(q, kc, vct, g3, jnp.asarray(ov.T, BF16))


def _flash_start(st, m_ref, l_ref, acc_ref, p_ref):
    m = jnp.max(st, axis=0, keepdims=True)
    m_ref[...] = m
    l_ref[...] = jnp.zeros(l_ref.shape, F32)
    acc_ref[...] = jnp.zeros(acc_ref.shape, F32)
    p_ref[...] = jnp.exp2(st - m).astype(BF16)


def _flash_step(st, vt_prev, m_ref, l_ref, acc_ref, p_ref):
    pv = _dot(vt_prev, p_ref[...])
    m_old = m_ref[...]
    m_new = jnp.maximum(m_old, jnp.max(st, axis=0, keepdims=True))
    alpha = jnp.exp2(m_old - m_new)
    l_ref[...] = (l_ref[...] + pv[LANE:LANE + 1]) * alpha
    acc_ref[...] = (acc_ref[...] + pv[0:LANE]) * alpha
    p_ref[...] = jnp.exp2(st - m_new).astype(BF16)
    m_ref[...] = m_new


def _flash_finish(vt_prev, l_ref, acc_ref, p_ref):
    pv = _dot(vt_prev, p_ref[...])
    return (acc_ref[...] + pv[0:LANE]) / (l_ref[...] + pv[LANE:LANE + 1])


def _tile_iotas(heads, t):
    key = lax.broadcasted_iota(jnp.int32, (t, heads * t), 0)
    qry = lax.broadcasted_iota(jnp.int32, (t, heads * t), 1) & (t - 1)
    return key, qry


def _nsa_attn_kernel(q_ref, selb_ref, ks_ref, vs_ref, kw_ref, vw_ref, gs_ref, gw_ref, ocmp_ref, o_out,
                     m_ref, l_ref, acc_ref, p_ref, *, t):
    qi = pl.program_id(2)
    key, qry = _tile_iotas(NSA_HPG, t)
    qs = [q_ref[:, hh * LANE:(hh + 1) * LANE] for hh in range(NSA_HPG)]
    q_win = jnp.concatenate(qs, axis=0)
    q_sel = jnp.concatenate([q + selb_ref[...] for q in qs], axis=0)
    low_rows = lax.broadcasted_iota(jnp.int32, (LANE, t), 0) < HEAD_DIM
    state = (m_ref, l_ref, acc_ref, p_ref)

    def keys(ref, j):
        return ref[pl.ds(pl.multiple_of(j * t, t), t), :]

    def finish(vt_prev):
        o = _flash_finish(vt_prev, l_ref, acc_ref, p_ref)
        return [jnp.where(low_rows, o[:, (2 * p) * t:(2 * p + 1) * t], o[:, (2 * p + 1) * t:(2 * p + 2) * t]).T
                for p in range(NSA_HPG // 2)]

    _flash_start(jnp.where(key <= qry, _dot_nt(keys(ks_ref, qi), q_sel), NEG), *state)

    def sel_body(j, prev):
        _flash_step(_dot_nt(keys(ks_ref, j), q_sel), vs_ref[prev], *state)
        return j

    prev = lax.fori_loop(0, qi, sel_body, qi)
    o_sel = finish(vs_ref[prev])

    _flash_start(jnp.where(key <= qry, _dot_nt(keys(kw_ref, qi), q_win), NEG), *state)

    @pl.when(qi >= 1)
    def _():
        _flash_step(_dot_nt(keys(kw_ref, qi - 1), q_win), vw_ref[qi], *state)

    @pl.when(qi >= 2)
    def _():
        s = _dot_nt(keys(kw_ref, qi - 2), q_win)
        _flash_step(jnp.where(key > qry, s, NEG), vw_ref[qi - 1], *state)

    o_win = finish(vw_ref[jnp.maximum(qi - 2, 0)])
    for p in range(NSA_HPG // 2):
        sl = slice(p * LANE, (p + 1) * LANE)
        o_out[:, sl] = (gs_ref[:, sl] * o_sel[p] + gw_ref[:, sl] * o_win[p] + ocmp_ref[:, sl]).astype(BF16)


def _nsa_attn(q, selb, ks, vs, kw, vw, g3, ocmp, batch, *, t=256):
    assert WINDOW == 2 * t
    n = q.shape[0]
    nq = SEQ // t
    gw = NSA_HPG * HEAD_DIM
    rows = lambda w, off=0: pl.BlockSpec((t, w), lambda b, g, i: (b * nq + i, g + off))
    seq = pl.BlockSpec((SEQ, LANE), lambda b, g, i: (b, g))
    seq_t = pl.BlockSpec((nq, VT_ROWS, t), lambda b, g, i: (b, g, 0))
    m = NSA_HPG * t
    return pl.pallas_call(
        functools.partial(_nsa_attn_kernel, t=t),
        grid=(batch, NSA_GROUPS, nq),
        in_specs=[rows(NSA_HPG * LANE), rows(LANE), seq, seq_t, seq, seq_t,
                  rows(gw, 2), rows(gw, 4), rows(gw)],
        out_specs=rows(gw),
        out_shape=jax.ShapeDtypeStruct((n, NSA_HEADS * HEAD_DIM), BF16),
        scratch_shapes=[pltpu.VMEM((1, m), F32), pltpu.VMEM((1, m), F32), pltpu.VMEM((LANE, m), F32),
                        pltpu.VMEM((t, m), BF16)],
        compiler_params=_cparams(("parallel", "parallel", "arbitrary")),
    )(q, selb, ks, vs, kw, vw, g3, g3, ocmp)


def _mla_attn_kernel(q_ref, k_ref, v_ref, o_out, m_ref, l_ref, acc_ref, p_ref, *, tq, tk):
    qi = pl.program_id(2)
    per = tq // tk
    key = lax.broadcasted_iota(jnp.int32, (tk, 2 * tq), 0)
    qry = lax.broadcasted_iota(jnp.int32, (tk, 2 * tq), 1) & (tq - 1)
    q0, q1 = q_ref[:, 0:LANE], q_ref[:, LANE:2 * LANE]

    def scores(j):
        k = k_ref[pl.ds(pl.multiple_of(j * tk, tk), tk), :]
        return jnp.concatenate([_dot_nt(k[:, 0:LANE], q0), _dot_nt(k[:, LANE:2 * LANE], q1)], axis=1)

    state = (m_ref, l_ref, acc_ref, p_ref)
    j0 = qi * per
    _flash_start(jnp.where(key <= qry, scores(j0), NEG), *state)
    for d in range(1, per):
        _flash_step(jnp.where(key + d * tk <= qry, scores(j0 + d), NEG), v_ref[j0 + d - 1], *state)

    def body(j, prev):
        _flash_step(scores(j), v_ref[prev], *state)
        return j

    prev = lax.fori_loop(0, j0, body, j0 + per - 1)
    o = _flash_finish(v_ref[prev], l_ref, acc_ref, p_ref)
    low_rows = lax.broadcasted_iota(jnp.int32, (LANE, tq), 0) < MLA_V
    o_out[...] = jnp.where(low_rows, o[:, 0:tq], o[:, tq:2 * tq]).T.astype(BF16)


def _mla_attn(qm, km, vm, batch, *, tq=512):
    n = qm.shape[0]
    nk, _, tk = vm.shape
    nq = SEQ // tq
    return pl.pallas_call(
        functools.partial(_mla_attn_kernel, tq=tq, tk=tk),
        grid=(batch, MLA_HEADS // 2, nq),
        in_specs=[pl.BlockSpec((tq, 2 * LANE), lambda b, p, i: (b * nq + i, p)),
                  pl.BlockSpec((SEQ, 2 * LANE), lambda b, p, i: (b, p)),
                  pl.BlockSpec((SEQ // tk, VT_ROWS, tk), lambda b, p, i: (b, p, 0))],
        out_specs=pl.BlockSpec((tq, LANE), lambda b, p, i: (b * nq + i, p)),
        out_shape=jax.ShapeDtypeStruct((n, MLA_HEADS * MLA_V), BF16),
        scratch_shapes=[pltpu.VMEM((1, 2 * tq), F32), pltpu.VMEM((1, 2 * tq), F32),
                        pltpu.VMEM((LANE, 2 * tq), F32), pltpu.VMEM((tk, 2 * tq), BF16)],
        compiler_params=_cparams(("parallel", "parallel", "arbitrary")),
    )(qm, km, vm)


def _mix_kernel(x_ref, on_ref, om_ref, gn_ref, gm_ref, wn_ref, wm_ref, wo_ref, g2_ref, wq_ref, k1_ref, k2_ref,
                h_out, xn_out, s_out, *, sub):
    subs = [slice(r0, r0 + sub) for r0 in range(0, x_ref.shape[0], sub)]
    merged = [gn_ref[s, :] * _dot(on_ref[s, :], wn_ref[...]) + gm_ref[s, :] * _dot(om_ref[s, :], wm_ref[...])
              for s in subs]
    hs = [x_ref[s, :] + _dot(mg.astype(BF16), wo_ref[...]) for s, mg in zip(subs, merged)]
    xns = []
    for s, h in zip(subs, hs):
        h_out[s, :] = h
        xn32 = _rms(h, g2_ref[...])
        xn_out[:, s] = xn32.T.astype(BF16)
        xns.append(xn32.astype(BF16))
    half = PEER_QDIM // 2
    for hd in range(PEER_HEADS):
        qs = [_dot(xn, wq_ref[:, hd * PEER_QDIM:(hd + 1) * PEER_QDIM]).astype(BF16) for xn in xns]
        for s, q in zip(subs, qs):
            s_out[2 * hd, :, s] = _dot_nt(k1_ref[...], q[:, 0:half])
            s_out[2 * hd + 1, :, s] = _dot_nt(k2_ref[...], q[:, half:PEER_QDIM])


def _mix(x2, o_nsa, o_mla, gn, gm, w_n, w_m, w_o, norm2_g, w_q, keys1, keys2, *, tm=512):
    n = x2.shape[0]
    row = lambda w: pl.BlockSpec((tm, w), lambda i: (i, 0))
    ws = [w_n.astype(BF16), w_m.astype(BF16), w_o.astype(BF16), norm2_g.reshape(1, -1), w_q.astype(BF16),
          keys1.astype(BF16), keys2.astype(BF16)]
    return pl.pallas_call(
        functools.partial(_mix_kernel, sub=256),
        grid=(n // tm,),
        in_specs=[row(D_MODEL), row(512), row(512), row(D_MODEL), row(D_MODEL)] + [_const_spec(w.shape) for w in ws],
        out_specs=[row(D_MODEL), pl.BlockSpec((D_MODEL, tm), lambda i: (0, i)),
                   pl.BlockSpec((2 * PEER_HEADS, PEER_KEYS, tm), lambda i: (0, 0, i))],
        out_shape=[jax.ShapeDtypeStruct((n, D_MODEL), F32), jax.ShapeDtypeStruct((D_MODEL, n), BF16),
                   jax.ShapeDtypeStruct((2 * PEER_HEADS, PEER_KEYS, n), F32)],
        compiler_params=_cparams(("parallel",)),
    )(x2, o_nsa, o_mla, gn, gm, *ws)


_CAND_ROWS = 16 + 7 * 8 + 8


def _take_top(v, k, idx, exact):
    n = v.shape[0]
    rank = jnp.full(v.shape, k, jnp.int32)
    vals = []
    for r in range(k):
        m = jnp.max(v, axis=0, keepdims=True)
        hit = v == m
        if exact:
            hit = idx == jnp.min(jnp.where(hit, idx, n), axis=0, keepdims=True)
        rank = jnp.where(hit, r, rank)
        v = jnp.where(hit, -jnp.inf, v)
        vals.append(m)
    return rank, vals


def _route(s1, s2, exact):
    k = PEER_TOPK
    tl = s1.shape[1]
    kidx = lax.broadcasted_iota(jnp.int32, (PEER_KEYS, tl), 0)
    rank1, a = _take_top(s1, k, kidx, exact)
    rank2, b = _take_top(s2, k, kidx, exact)
    bcat = jnp.concatenate(b, axis=0)
    groups = [a[0] + bcat]
    for p in range(1, 8):
        groups.append(a[p] + bcat[0:8])
    groups.append(jnp.concatenate(a[8:16], axis=0) + b[0])
    cand = jnp.concatenate(groups, axis=0)
    ridx = lax.broadcasted_iota(jnp.int32, (_CAND_ROWS, tl), 0)
    qpos = jnp.where(ridx < 16, ridx, (ridx - 16) & 7)
    ppos = jnp.where(ridx < 16, 0, ((ridx - 16) >> 3) + 1)
    valid = (ridx >= 16 + 7 * 8) | ((ppos + 1) * (qpos + 1) <= k)
    crank, _ = _take_top(jnp.where(valid, cand, -jnp.inf), k, ridx, exact)
    taken = crank < k
    takenf = taken.astype(F32)
    z = jnp.sum(jnp.where(taken, jnp.exp(cand - cand[0:1]), 0.0), axis=0, keepdims=True)
    cnt = jnp.zeros((PEER_KEYS, tl), F32)
    for p in range(k):
        if p == 0:
            c = jnp.sum(takenf[0:16], axis=0, keepdims=True)
        elif p < 8:
            c = jnp.sum(takenf[16 + (p - 1) * 8:16 + p * 8], axis=0, keepdims=True)
        else:
            c = takenf[72 + (p - 8):72 + (p - 8) + 1]
        cnt = jnp.where(rank1 == p, c, cnt)
    ea = jnp.where(rank1 < k, jnp.exp(s1 - a[0]), 0.0) / z
    eb = jnp.where(rank2 < k, jnp.exp(s2 - b[0]), 0.0)
    n_taken = (jnp.sum((rank1 < k).astype(F32), axis=0, keepdims=True)
               + jnp.sum((rank2 < k).astype(F32), axis=0, keepdims=True)
               + jnp.sum(takenf, axis=0, keepdims=True))
    return (cnt, ea, rank2.astype(F32).astype(BF16), eb.astype(BF16)), jnp.max(n_taken)


def _peer_topk_kernel(s_ref, cnt_out, ea_out, r2_out, eb_out):
    s1, s2 = s_ref[0], s_ref[1]
    outs = (cnt_out, ea_out, r2_out, eb_out)
    res, n_taken = _route(s1, s2, exact=False)
    for o, r in zip(outs, res):
        o[0] = r

    @pl.when(n_taken > 3 * PEER_TOPK)
    def _():
        res, _ = _route(s1, s2, exact=True)
        for o, r in zip(outs, res):
            o[0] = r


def _peer_topk(scores, *, tl=512):
    n = scores.shape[-1]
    ospec = pl.BlockSpec((1, PEER_KEYS, tl), lambda i, h: (h, 0, i))
    shp = (PEER_HEADS, PEER_KEYS, n)
    return pl.pallas_call(
        _peer_topk_kernel,
        grid=(n // tl, PEER_HEADS),
        in_specs=[pl.BlockSpec((2, PEER_KEYS, tl), lambda i, h: (h, 0, i))],
        out_specs=[ospec] * 4,
        out_shape=[jax.ShapeDtypeStruct(shp, F32), jax.ShapeDtypeStruct(shp, F32),
                   jax.ShapeDtypeStruct(shp, BF16), jax.ShapeDtypeStruct(shp, BF16)],
        compiler_params=_cparams(("parallel", "parallel")),
    )(scores)


def _peer_dense_kernel(xnt_ref, d_ref, ut_ref, cnt_ref, ea_ref, r2_ref, eb_ref, y_out, g_ref, *, rows_per_step):
    e = pl.program_id(1)
    tl = xnt_ref.shape[1]
    chunk = 2 * PEER_KEYS

    @pl.when(e == 0)
    def _():
        y_out[...] = jnp.zeros(y_out.shape, F32)

    for c in range(rows_per_step // 2):
        hid = _dot(d_ref[c * chunk:(c + 1) * chunk, :], xnt_ref[...])
        for k2 in range(2):
            ii = 2 * c + k2
            w = None
            for hd in range(PEER_HEADS):
                cnt = jnp.broadcast_to(cnt_ref[hd, ii:ii + 1, :], (BF16_ROWS, tl)).astype(BF16)
                ea = jnp.broadcast_to(ea_ref[hd, ii:ii + 1, :], (BF16_ROWS, tl)).astype(BF16)
                wh = jnp.where(r2_ref[hd] < cnt[None], eb_ref[hd] * ea[None], jnp.zeros((), BF16))
                w = wh if w is None else w + wh
            act = _gelu_tanh(hid[k2 * PEER_KEYS:(k2 + 1) * PEER_KEYS, :]).astype(BF16)
            g_ref[ii * PEER_KEYS:(ii + 1) * PEER_KEYS, :] = act * w.reshape(PEER_KEYS, tl)
    y_out[...] += _dot(ut_ref[...], g_ref[...])


def _peer_dense(xnt, d_bf, ut_bf, cnt, ea, r2, eb, *, tl=1024, eb_rows=16):
    n = xnt.shape[1]
    ebk = eb_rows * PEER_KEYS
    groups = PEER_KEYS // BF16_ROWS
    split = lambda a: a.reshape(PEER_HEADS, groups, BF16_ROWS, n)
    full = pl.BlockSpec((PEER_HEADS, groups, BF16_ROWS, tl), lambda i, e: (0, 0, 0, i))
    part = pl.BlockSpec((PEER_HEADS, eb_rows, tl), lambda i, e: (0, e, i))
    return pl.pallas_call(
        functools.partial(_peer_dense_kernel, rows_per_step=eb_rows),
        grid=(n // tl, PEER_EXPERTS // ebk),
        in_specs=[pl.BlockSpec((D_MODEL, tl), lambda i, e: (0, i)),
                  pl.BlockSpec((ebk, D_MODEL), lambda i, e: (e, 0)),
                  pl.BlockSpec((D_MODEL, ebk), lambda i, e: (0, e)),
                  part, part, full, full],
        out_specs=pl.BlockSpec((D_MODEL, tl), lambda i, e: (0, i)),
        out_shape=jax.ShapeDtypeStruct((D_MODEL, n), F32),
        scratch_shapes=[pltpu.VMEM((ebk, tl), BF16)],
        compiler_params=_cparams(("parallel", "arbitrary")),
    )(xnt, d_bf, ut_bf, cnt, ea, split(r2), split(eb))


def _final_kernel(h_ref, yt_ref, g_ref, o_out):
    o_out[...] = _rms(h_ref[...] + yt_ref[...].T, g_ref[...])


def _final(h, yt, norm_f_g, *, tm=256):
    n = h.shape[0]
    return pl.pallas_call(
        _final_kernel,
        grid=(n // tm,),
        in_specs=[pl.BlockSpec((tm, D_MODEL), lambda i: (i, 0)), pl.BlockSpec((D_MODEL, tm), lambda i: (0, i)),
                  _const_spec((1, D_MODEL))],
        out_specs=pl.BlockSpec((tm, D_MODEL), lambda i: (i, 0)),
        out_shape=jax.ShapeDtypeStruct((n, D_MODEL), F32),
        compiler_params=_cparams(("parallel",)),
    )(h, yt, norm_f_g.reshape(1, -1))


def _layer(h2, pos2, batch, p):
    a = _inproj(h2, pos2, p['norm1_g'], p['w_in'], p['mla_q_norm_g'], p['mla_kv_norm_g'], p['mla_w_uq'],
                p['mla_w_ukv'])
    kc, vc = _compress(a['kc'], a['vc'], p['nsa_cmp_pos_k'], p['nsa_cmp_w1_k'], p['nsa_cmp_w2_k'],
                       p['nsa_cmp_pos_v'], p['nsa_cmp_w1_v'], p['nsa_cmp_w2_v'], batch)
    ocmp, selb = _nsa_cmp(a['q'], kc, vc, a['g3'], batch)
    o_nsa = _nsa_attn(a['q'], selb, a['ks'], a['vs'], a['kw'], a['vw'], a['g3'], ocmp, batch)
    o_mla = _mla_attn(a['qm'], a['km'], a['vm'], batch)
    h_mid, xn, scores = _mix(h2, o_nsa, o_mla, a['gn'], a['gm'], p['w_branch_nsa'], p['w_branch_mla'], p['w_out'],
                             p['norm2_g'], p['peer_w_query'], p['peer_sub_keys_1'], p['peer_sub_keys_2'])
    cnt, ea, r2, eb = _peer_topk(scores)
    yt = _peer_dense(xn, p['peer_expert_down'].astype(BF16), p['peer_expert_up'].T.astype(BF16), cnt, ea, r2, eb)
    return h_mid, yt


def kernel(x, positions, norm1_g, w_in, nsa_cmp_pos_k, nsa_cmp_w1_k, nsa_cmp_w2_k, nsa_cmp_pos_v, nsa_cmp_w1_v,
           nsa_cmp_w2_v, mla_q_norm_g, mla_kv_norm_g, mla_w_uq, mla_w_ukv, w_branch_nsa, w_branch_mla, w_out,
           norm2_g, peer_w_query, peer_sub_keys_1, peer_sub_keys_2, peer_expert_down, peer_expert_up, norm_f_g):
    batch, seq, d = x.shape
    assert seq == SEQ and d == D_MODEL
    stacked = dict(norm1_g=norm1_g, w_in=w_in, nsa_cmp_pos_k=nsa_cmp_pos_k, nsa_cmp_w1_k=nsa_cmp_w1_k,
                   nsa_cmp_w2_k=nsa_cmp_w2_k, nsa_cmp_pos_v=nsa_cmp_pos_v, nsa_cmp_w1_v=nsa_cmp_w1_v,
                   nsa_cmp_w2_v=nsa_cmp_w2_v, mla_q_norm_g=mla_q_norm_g, mla_kv_norm_g=mla_kv_norm_g,
                   mla_w_uq=mla_w_uq, mla_w_ukv=mla_w_ukv, w_branch_nsa=w_branch_nsa, w_branch_mla=w_branch_mla,
                   w_out=w_out, norm2_g=norm2_g, peer_w_query=peer_w_query, peer_sub_keys_1=peer_sub_keys_1,
                   peer_sub_keys_2=peer_sub_keys_2, peer_expert_down=peer_expert_down,
                   peer_expert_up=peer_expert_up)
    assert w_in.shape[0] == 1, "single-layer block"
    p = {k: v[0] for k, v in stacked.items()}
    h_mid, yt = _layer(x.reshape(batch * seq, d), positions.reshape(batch * seq, 1), batch, p)
    return _final(h_mid, yt, norm_f_g).reshape(batch, seq, d)
```
